```python
import jax, jax.numpy as jnp
from jax import lax
import numpy as np

D_MODEL = 2048
BATCH = 2
SEQ = 16384
DEPTH = 2

HEAD_DIM = 128
ATTN_GROUPS = ((128, 1), (512, 4), (2048, 16))
ATTN_HEADS_PER_GROUP = 4
N_ATTN_GROUPS = len(ATTN_GROUPS)
N_ATTN_HEADS = N_ATTN_GROUPS * ATTN_HEADS_PER_GROUP
ATTN_WIDTH = N_ATTN_HEADS * HEAD_DIM
ATTN_OUT_WIDTH = ATTN_HEADS_PER_GROUP * HEAD_DIM
BLOCK = 128
ROPE_THETA = 10000.0
NEG_INF = -1e30

N_RET_HEADS = D_MODEL // 256
RET_QK_DIM = 256
RET_V_DIM = 512
RET_QK_WIDTH = N_RET_HEADS * RET_QK_DIM
RET_V_WIDTH = N_RET_HEADS * RET_V_DIM
CHUNK = 128
GN_EPS = 1e-5

D_FF = 5632
RMS_EPS = 1e-6

IN_SPLITS = (ATTN_WIDTH, ATTN_WIDTH, ATTN_WIDTH,
             RET_QK_WIDTH, RET_QK_WIDTH, RET_V_WIDTH, RET_V_WIDTH,
             D_MODEL, D_MODEL)
IN_COLS = sum(IN_SPLITS)

kernel_name = "hybrid_dilated_attn_retention_macaron"


def rms_norm(x, g):
    xf = x.astype(jnp.float32)
    y = xf * lax.rsqrt(jnp.mean(xf * xf, axis=-1, keepdims=True) + RMS_EPS)
    return (y * g.astype(jnp.float32)).astype(x.dtype)


def swiglu(h, w_gate, w_up, w_down):
    return (jax.nn.silu(h @ w_gate) * (h @ w_up)) @ w_down


def rope_tables(seq, dim):
    inv_freq = 1.0 / (ROPE_THETA ** (jnp.arange(0, dim, 2, dtype=jnp.float32) / dim))
    ang = jnp.arange(seq, dtype=jnp.float32)[:, None] * inv_freq[None, :]
    return jnp.cos(ang), jnp.sin(ang)


def apply_rotary(x, cos, sin):
    xf = x.astype(jnp.float32)
    half = xf.shape[-1] // 2
    x1, x2 = xf[..., :half], xf[..., half:]
    c, s = cos[:, None, :], sin[:, None, :]
    return jnp.concatenate([x1 * c - x2 * s, x2 * c + x1 * s], axis=-1)


def to_strided(x, dilation, nb):
    b, rest = x.shape[0], x.shape[2:]
    return x.reshape(b, nb * BLOCK, dilation, *rest).swapaxes(1, 2).reshape(b, dilation, nb, BLOCK, *rest)


def from_strided(x, seq):
    b, d, nb, blk = x.shape[:4]
    rest = x.shape[4:]
    return x.reshape(b, d, nb * blk, *rest).swapaxes(1, 2).reshape(b, nb * blk * d, *rest)[:, :seq]


def dilated_group_attention(q, k, v, window, dilation):
    steps = window // dilation
    b, s, h, dh = q.shape
    sub_len = -(-s // dilation)
    nb = -(-sub_len // BLOCK)
    sp = nb * BLOCK * dilation
    pad = ((0, 0), (0, sp - s), (0, 0), (0, 0))
    qs = to_strided(jnp.pad(q, pad), dilation, nb)
    ks = to_strided(jnp.pad(k, pad), dilation, nb)
    vs = to_strided(jnp.pad(v.astype(jnp.float32), pad), dilation, nb)
    prev = lambda t: jnp.pad(t, ((0, 0), (0, 0), (1, 0), (0, 0), (0, 0), (0, 0)))[:, :, :-1]
    kc = jnp.concatenate([prev(ks), ks], axis=3)
    vc = jnp.concatenate([prev(vs), vs], axis=3)
    scores = jnp.einsum('brnqhd,brnkhd->brnhqk', qs, kc)
    qi = jnp.arange(BLOCK)[:, None]
    kj = jnp.arange(2 * BLOCK)[None, :]
    blk = jnp.arange(nb)[:, None, None]
    dist = BLOCK + qi - kj
    valid = (dist >= 0) & (dist <= steps) & (blk * BLOCK + kj - BLOCK >= 0)
    valid = valid[None, None, :, None]
    scores = jnp.where(valid, scores, NEG_INF)
    m = jnp.max(scores, axis=-1, keepdims=True)
    p = jnp.where(valid, jnp.exp(scores - m), 0.0)
    l = jnp.sum(p, axis=-1)
    l_t = jnp.swapaxes(l, -1, -2)
    o = jnp.einsum('brnhqk,brnkhe->brnqhe', p, vc) / l_t[..., None]
    o = from_strided(o, s)
    m = from_strided(jnp.swapaxes(m[..., 0], -1, -2), s)
    l = from_strided(l_t, s)
    return o, m, l


def dilated_attention_mixer(q, k, v, cos, sin):
    b, s, _ = q.shape
    q = apply_rotary(q.reshape(b, s, N_ATTN_HEADS, HEAD_DIM), cos, sin) * (HEAD_DIM ** -0.5)
    k = apply_rotary(k.reshape(b, s, N_ATTN_HEADS, HEAD_DIM), cos, sin)
    v = v.reshape(b, s, N_ATTN_HEADS, HEAD_DIM)
    q = q.reshape(b, s, N_ATTN_GROUPS, ATTN_HEADS_PER_GROUP, HEAD_DIM)
    k = k.reshape(b, s, N_ATTN_GROUPS, ATTN_HEADS_PER_GROUP, HEAD_DIM)
    v = v.reshape(b, s, N_ATTN_GROUPS, ATTN_HEADS_PER_GROUP, HEAD_DIM)
    outs, maxes, dens = [], [], []
    for g, (window, dilation) in enumerate(ATTN_GROUPS):
        o, m, l = dilated_group_attention(q[:, :, g], k[:, :, g], v[:, :, g], window, dilation)
        outs.append(o)
        maxes.append(m)
        dens.append(l)
    o = jnp.stack(outs)
    m = jnp.stack(maxes)
    l = jnp.stack(dens)
    weight = l * jnp.exp(m - jnp.max(m, axis=0, keepdims=True))
    o = jnp.sum(weight[..., None] * o, axis=0) / jnp.sum(weight, axis=0)[..., None]
    return o.reshape(b, s, ATTN_OUT_WIDTH)


def retention_mixer(q, k, v, g, cos, sin):
    b, s, _ = q.shape
    n = s // CHUNK
    log_gamma = jnp.log1p(-jnp.exp2(-5.0 - jnp.arange(N_RET_HEADS, dtype=jnp.float32)))
    q = apply_rotary(q.reshape(b, s, N_RET_HEADS, RET_QK_DIM), cos, sin)
    k = apply_rotary(k.reshape(b, s, N_RET_HEADS, RET_QK_DIM), cos, sin) * (RET_QK_DIM ** -0.5)
    v = v.astype(jnp.float32).reshape(b, s, N_RET_HEADS, RET_V_DIM)
    qc = q.reshape(b, n, CHUNK, N_RET_HEADS, RET_QK_DIM)
    kc = k.reshape(b, n, CHUNK, N_RET_HEADS, RET_QK_DIM)
    vc = v.reshape(b, n, CHUNK, N_RET_HEADS, RET_V_DIM)
    idx = jnp.arange(CHUNK, dtype=jnp.float32)
    rel = idx[:, None] - idx[None, :]
    decay_mask = jnp.where(rel >= 0, jnp.exp(log_gamma[:, None, None] * jnp.maximum(rel, 0.0)), 0.0)
    inner = jnp.einsum('bnihd,bnjhd->bnhij', qc, kc) * decay_mask
    o_inner = jnp.einsum('bnhij,bnjhe->bnihe', inner, vc)
    q_dec = qc * jnp.exp(log_gamma[None, :] * (idx + 1.0)[:, None])[:, :, None]
    k_dec = kc * jnp.exp(log_gamma[None, :] * (CHUNK - 1.0 - idx)[:, None])[:, :, None]
    chunk_decay = jnp.exp(log_gamma * CHUNK)[None, :, None, None]

    def step(state, xs):
        qn, kn, vn = xs
        out = jnp.einsum('bihd,bhde->bihe', qn, state)
        state = chunk_decay * state + jnp.einsum('bjhd,bjhe->bhde', kn, vn)
        return state, out

    state0 = jnp.zeros((b, N_RET_HEADS, RET_QK_DIM, RET_V_DIM), jnp.float32)
    _, o_cross = lax.scan(step, state0, (q_dec.swapaxes(0, 1), k_dec.swapaxes(0, 1), vc.swapaxes(0, 1)))
    o = (o_inner + o_cross.swapaxes(0, 1)).reshape(b, s, N_RET_HEADS, RET_V_DIM)
    mu = jnp.mean(o, axis=-1, keepdims=True)
    var = jnp.mean(jnp.square(o - mu), axis=-1, keepdims=True)
    o = (o - mu) * lax.rsqrt(var + GN_EPS)
    gate = jax.nn.silu(g.astype(jnp.float32)).reshape(b, s, N_RET_HEADS, RET_V_DIM)
    return (o * gate).reshape(b, s, RET_V_WIDTH)


def setup_inputs(seed: int = 0) -> dict:
    key = jax.random.key(seed)
    ks = jax.random.split(key, 16)

    def w(k, shape, fan_in):
        return jax.random.normal(k, shape, jnp.float32) * (fan_in ** -0.5)

    def gain(k, shape):
        return 1.0 + 0.01 * jax.random.normal(k, shape, jnp.float32)

    return {
        "x": jax.random.normal(ks[0], (BATCH, SEQ, D_MODEL), jnp.float32),
        "ffn1_norm": gain(ks[1], (DEPTH, D_MODEL)),
        "ffn1_w_gate": w(ks[2], (DEPTH, D_MODEL, D_FF), D_MODEL),
        "ffn1_w_up": w(ks[3], (DEPTH, D_MODEL, D_FF), D_MODEL),
        "ffn1_w_down": w(ks[4], (DEPTH, D_FF, D_MODEL), D_FF),
        "mix_norm": gain(ks[5], (DEPTH, D_MODEL)),
        "w_in": w(ks[6], (DEPTH, D_MODEL, IN_COLS), D_MODEL),
        "w_proj_attn": w(ks[7], (DEPTH, ATTN_OUT_WIDTH, D_MODEL), ATTN_OUT_WIDTH),
        "w_proj_ret": w(ks[8], (DEPTH, RET_V_WIDTH, D_MODEL), RET_V_WIDTH),
        "w_out": w(ks[9], (DEPTH, D_MODEL, D_MODEL), D_MODEL),
        "ffn2_norm": gain(ks[10], (DEPTH, D_MODEL)),
        "ffn2_w_gate": w(ks[11], (DEPTH, D_MODEL, D_FF), D_MODEL),
        "ffn2_w_up": w(ks[12], (DEPTH, D_MODEL, D_FF), D_MODEL),
        "ffn2_w_down": w(ks[13], (DEPTH, D_FF, D_MODEL), D_FF),
        "final_norm": gain(ks[14], (D_MODEL,)),
    }


def reference(x, ffn1_norm, ffn1_w_gate, ffn1_w_up, ffn1_w_down, mix_norm, w_in,
              w_proj_attn, w_proj_ret, w_out, ffn2_norm, ffn2_w_gate, ffn2_w_up,
              ffn2_w_down, final_norm):
    seq = x.shape[1]
    cos_a, sin_a = rope_tables(seq, HEAD_DIM)
    cos_r, sin_r = rope_tables(seq, RET_QK_DIM)
    offsets = [int(o) for o in np.cumsum(IN_SPLITS)[:-1]]
    for layer in range(DEPTH):
        x = x + 0.5 * swiglu(rms_norm(x, ffn1_norm[layer]), ffn1_w_gate[layer],
                             ffn1_w_up[layer], ffn1_w_down[layer])
        h = rms_norm(x, mix_norm[layer])
        proj = h @ w_in[layer]
        q_a, k_a, v_a, q_r, k_r, v_r, g_r, gate_a, gate_b = jnp.split(proj, offsets, axis=-1)
        o_a = dilated_attention_mixer(q_a, k_a, v_a, cos_a, sin_a).astype(x.dtype)
        o_r = retention_mixer(q_r, k_r, v_r, g_r, cos_r, sin_r).astype(x.dtype)
        merged = (jax.nn.sigmoid(gate_a) * (o_a @ w_proj_attn[layer])
                  + jax.nn.sigmoid(gate_b) * (o_r @ w_proj_ret[layer]))
        x = x + merged @ w_out[layer]
        x = x + 0.5 * swiglu(rms_norm(x, ffn2_norm[layer]), ffn2_w_gate[layer],
                             ffn2_w_up[layer], ffn2_w_down[layer])
    return rms_norm(x, final_norm)
```

```python
import functools

import jax
import jax.numpy as jnp
import numpy as np
from jax import lax
from jax.experimental import pallas as pl
from jax.experimental.pallas import tpu as pltpu

F32 = jnp.float32
BF16 = jnp.bfloat16

LANES = 128
VMEM_LIMIT_BYTES = 56 * 1024 * 1024

D_MODEL = 2048
HEAD_DIM = 128
ATTN_GROUPS = ((128, 1), (512, 4), (2048, 16))
ATTN_HEADS_PER_GROUP = 4
N_ATTN_HEADS = len(ATTN_GROUPS) * ATTN_HEADS_PER_GROUP
ATTN_WIDTH = N_ATTN_HEADS * HEAD_DIM
ATTN_OUT_WIDTH = ATTN_HEADS_PER_GROUP * HEAD_DIM
BLOCK = 128
ROPE_THETA = 10000.0
NEG_INF = -1e30
N_RET_HEADS = D_MODEL // 256
RET_QK_DIM = 256
RET_V_DIM = 512
RET_QK_WIDTH = N_RET_HEADS * RET_QK_DIM
RET_V_WIDTH = N_RET_HEADS * RET_V_DIM
CHUNK = 128
GN_EPS = 1e-5
D_FF = 5632
RMS_EPS = 1e-6
IN_SPLITS = (ATTN_WIDTH, ATTN_WIDTH, ATTN_WIDTH, RET_QK_WIDTH, RET_QK_WIDTH,
             RET_V_WIDTH, RET_V_WIDTH, D_MODEL, D_MODEL)
IN_OFFSETS = tuple(int(o) for o in np.cumsum((0,) + IN_SPLITS[:-1]))

ATTN_SUPER = BLOCK * ATTN_GROUPS[-1][1]


def _params(*sem):
    return pltpu.CompilerParams(dimension_semantics=sem, vmem_limit_bytes=VMEM_LIMIT_BYTES)


def _mm(a, b):
    return jnp.dot(a, b, preferred_element_type=F32)


def _mm_nt(a, b):
    return lax.dot_general(a, b, (((1,), (1,)), ((), ())), preferred_element_type=F32)


def _rms(x, g):
    return x * lax.rsqrt(jnp.mean(x * x, axis=-1, keepdims=True) + RMS_EPS) * g


def _rmsnorm_kernel(x_ref, g_ref, o_ref):
    o_ref[...] = _rms(x_ref[...], g_ref[...]).astype(o_ref.dtype)


def rmsnorm(x, g, out_dtype, tm=512):
    t, d = x.shape
    return pl.pallas_call(
        _rmsnorm_kernel,
        grid=(t // tm,),
        in_specs=[pl.BlockSpec((tm, d), lambda i: (i, 0)),
                  pl.BlockSpec((1, d), lambda i: (0, 0))],
        out_specs=pl.BlockSpec((tm, d), lambda i: (i, 0)),
        out_shape=jax.ShapeDtypeStruct((t, d), out_dtype),
        compiler_params=_params("parallel"),
        name="rmsnorm",
    )(x, g.reshape(1, d))


def _ffn_kernel(x_ref, g_ref, wg_ref, wu_ref, wd_ref, o_ref, h_ref, acc_ref):
    f = pl.program_id(1)

    @pl.when(f == 0)
    def _():
        h_ref[...] = _rms(x_ref[...], g_ref[...]).astype(BF16)
        acc_ref[...] = jnp.zeros_like(acc_ref)

    h = h_ref[...]
    gate = _mm(h, wg_ref[...])
    up = _mm(h, wu_ref[...])
    act = gate * jax.nn.sigmoid(gate) * up
    acc_ref[...] += _mm(act.astype(BF16), wd_ref[...])

    @pl.when(f == pl.num_programs(1) - 1)
    def _():
        o_ref[...] = x_ref[...] + 0.5 * acc_ref[...]


def ffn(x, g, w_gate, w_up, w_down, tm=512, tf=512):
    t, d = x.shape
    dff = w_gate.shape[1]
    return pl.pallas_call(
        _ffn_kernel,
        grid=(t // tm, dff // tf),
        in_specs=[pl.BlockSpec((tm, d), lambda i, f: (i, 0)),
                  pl.BlockSpec((1, d), lambda i, f: (0, 0)),
                  pl.BlockSpec((d, tf), lambda i, f: (0, f)),
                  pl.BlockSpec((d, tf), lambda i, f: (0, f)),
                  pl.BlockSpec((tf, d), lambda i, f: (f, 0))],
        out_specs=pl.BlockSpec((tm, d), lambda i, f: (i, 0)),
        out_shape=jax.ShapeDtypeStruct((t, d), F32),
        scratch_shapes=[pltpu.VMEM((tm, d), BF16), pltpu.VMEM((tm, d), F32)],
        compiler_params=_params("parallel", "arbitrary"),
        name="ffn",
    )(x, g.reshape(1, d), w_gate, w_up, w_down)


def _proj_kernel(*refs, mode, scale, tn):
    if mode in ("rot128", "rot256"):
        a_ref, w_ref, c_ref, s_ref, o_ref = refs
    elif mode == "residual":
        a_ref, w_ref, x_ref, o_ref = refs
    else:
        a_ref, w_ref, o_ref = refs
    y = _mm(a_ref[...], w_ref[...])
    if scale is not None:
        y = y * scale
    if mode == "rot128":
        c, s = c_ref[...], s_ref[...]
        for b in range(tn // LANES):
            yb = y[:, b * LANES:(b + 1) * LANES]
            ob = yb * c + pltpu.roll(yb, LANES // 2, axis=1) * s
            o_ref[:, b * LANES:(b + 1) * LANES] = ob.astype(o_ref.dtype)
    elif mode == "rot256":
        c, s = c_ref[...], s_ref[...]
        for b in range(tn // (2 * LANES)):
            y1 = y[:, (2 * b) * LANES:(2 * b + 1) * LANES]
            y2 = y[:, (2 * b + 1) * LANES:(2 * b + 2) * LANES]
            o_ref[:, (2 * b) * LANES:(2 * b + 1) * LANES] = (y1 * c - y2 * s).astype(o_ref.dtype)
            o_ref[:, (2 * b + 1) * LANES:(2 * b + 2) * LANES] = (y2 * c + y1 * s).astype(o_ref.dtype)
    elif mode == "silu":
        o_ref[...] = (y * jax.nn.sigmoid(y)).astype(o_ref.dtype)
    elif mode == "sigmoid":
        o_ref[...] = jax.nn.sigmoid(y).astype(o_ref.dtype)
    elif mode == "residual":
        o_ref[...] = (x_ref[...] + y).astype(o_ref.dtype)
    else:
        o_ref[...] = y.astype(o_ref.dtype)


def proj(a, w, col_off, n_cols, mode, out_dtype, *, tables=None, scale=None, resid=None,
         seq=None, tm=1024, tn=512):
    t, k = a.shape
    jb = col_off // tn
    in_specs = [pl.BlockSpec((tm, k), lambda i, j: (i, 0)),
                pl.BlockSpec((k, tn), lambda i, j: (0, jb + j))]
    args = [a, w]
    if mode in ("rot128", "rot256"):
        nsb = seq // tm
        tab_spec = pl.BlockSpec((tm, LANES), lambda i, j: (i % nsb, 0))
        in_specs += [tab_spec, tab_spec]
        args += list(tables)
    elif mode == "residual":
        in_specs.append(pl.BlockSpec((tm, tn), lambda i, j: (i, j)))
        args.append(resid)
    return pl.pallas_call(
        functools.partial(_proj_kernel, mode=mode, scale=scale, tn=tn),
        grid=(t // tm, n_cols // tn),
        in_specs=in_specs,
        out_specs=pl.BlockSpec((tm, tn), lambda i, j: (i, j)),
        out_shape=jax.ShapeDtypeStruct((t, n_cols), out_dtype),
        compiler_params=_params("parallel", "arbitrary"),
        name="proj_" + mode,
    )(*args)


def _attn_kernel(q0, q1, q2, k0, k1, k2, v0, v1, v2, kp0, kp1, kp2, vp0, vp1, vp2,
                 o_ref, a0, a1, a2, m0, m1, m2, l0, l1, l2):
    row = lax.broadcasted_iota(jnp.int32, (BLOCK, BLOCK), 0)
    col = lax.broadcasted_iota(jnp.int32, (BLOCK, BLOCK), 1)
    mask_c = col <= row
    mask_p = col >= row
    mask_p_halo = col >= row + jnp.where(pl.program_id(2) > 0, 0, BLOCK)

    def tile(q, kc, vc, kp, vp, mp):
        qb = q.astype(BF16)
        s_c = jnp.where(mask_c, _mm_nt(qb, kc.astype(BF16)), NEG_INF)
        s_p = jnp.where(mp, _mm_nt(qb, kp.astype(BF16)), NEG_INF)
        m = jnp.maximum(jnp.max(s_c, axis=1, keepdims=True), jnp.max(s_p, axis=1, keepdims=True))
        p_c = jnp.where(mask_c, jnp.exp(s_c - m), 0.0)
        p_p = jnp.where(mp, jnp.exp(s_p - m), 0.0)
        l = jnp.sum(p_c, axis=1, keepdims=True) + jnp.sum(p_p, axis=1, keepdims=True)
        acc = _mm(p_c.astype(BF16), vc.astype(BF16)) + _mm(p_p.astype(BF16), vp.astype(BF16))
        return acc, m, l

    def put(a_ref, m_ref, l_ref, rows, res):
        acc, m, l = res
        a_ref[rows, :] = acc
        m_ref[rows, :] = jnp.broadcast_to(m, (BLOCK, LANES))
        l_ref[rows, :] = jnp.broadcast_to(l, (BLOCK, LANES))

    def group(q, k, v, kp, vp, a_ref, m_ref, l_ref, d):
        span = BLOCK * d
        n_sb = ATTN_SUPER // span

        def rows_at(start):
            return pl.ds(start, BLOCK, stride=d) if d > 1 else pl.ds(start, BLOCK)

        def first(r, carry):
            rows = rows_at(r)
            put(a_ref, m_ref, l_ref, rows, tile(q[rows, :], k[rows, :], v[rows, :],
                                                kp[rows, :], vp[rows, :], mask_p_halo))
            return carry

        lax.fori_loop(0, d, first, 0)

        def rest(sb, carry):
            base = pl.multiple_of(sb * span, span)

            def residue(r, carry):
                rows, prev = rows_at(base + r), rows_at(base - span + r)
                put(a_ref, m_ref, l_ref, rows, tile(q[rows, :], k[rows, :], v[rows, :],
                                                    k[prev, :], v[prev, :], mask_p))
                return carry

            return lax.fori_loop(0, d, residue, carry)

        if n_sb > 1:
            lax.fori_loop(1, n_sb, rest, 0)

    group(q0, k0, v0, kp0, vp0, a0, m0, l0, ATTN_GROUPS[0][1])
    group(q1, k1, v1, kp1, vp1, a1, m1, l1, ATTN_GROUPS[1][1])
    group(q2, k2, v2, kp2, vp2, a2, m2, l2, ATTN_GROUPS[2][1])

    rc = 256

    def merge(i, carry):
        rows = pl.ds(pl.multiple_of(i * rc, rc), rc)
        ma, mb, mc = m0[rows, :], m1[rows, :], m2[rows, :]
        mx = jnp.maximum(jnp.maximum(ma, mb), mc)
        ea, eb, ec = jnp.exp(ma - mx), jnp.exp(mb - mx), jnp.exp(mc - mx)
        num = ea * a0[rows, :] + eb * a1[rows, :] + ec * a2[rows, :]
        den = ea * l0[rows, :] + eb * l1[rows, :] + ec * l2[rows, :]
        o_ref[rows, :] = (num / den).astype(o_ref.dtype)
        return carry

    lax.fori_loop(0, ATTN_SUPER // rc, merge, 0)


def attention(q, k, v, out_dtype):
    b, s, _ = q.shape
    hg = ATTN_HEADS_PER_GROUP
    nsb = s // ATTN_SUPER

    def cur(g):
        return pl.BlockSpec((None, ATTN_SUPER, HEAD_DIM), lambda bi, h, j: (bi, j, g * hg + h))

    def prev(g):
        span = BLOCK * ATTN_GROUPS[g][1]
        per = ATTN_SUPER // span
        return pl.BlockSpec((None, span, HEAD_DIM),
                            lambda bi, h, j: (bi, jnp.maximum(j * per - 1, 0), g * hg + h))

    in_specs = ([cur(g) for g in range(3)] * 3 + [prev(g) for g in range(3)] * 2)
    scratch = [pltpu.VMEM((ATTN_SUPER, LANES), F32) for _ in range(9)]
    return pl.pallas_call(
        _attn_kernel,
        grid=(b, hg, nsb),
        in_specs=in_specs,
        out_specs=pl.BlockSpec((None, ATTN_SUPER, HEAD_DIM), lambda bi, h, j: (bi, j, h)),
        out_shape=jax.ShapeDtypeStruct((b, s, ATTN_OUT_WIDTH), out_dtype),
        scratch_shapes=scratch,
        compiler_params=_params("parallel", "parallel", "arbitrary"),
        name="dilated_attn",
    )(q, q, q, k, k, k, v, v, v, k, k, k, v, v, v)


def _ret_kernel(q_ref, k_ref, v_ref, g_ref, dm_ref, qd_ref, kd_ref, cd_ref, o_ref, state_ref, *, n_chunks):
    @pl.when(pl.program_id(2) == 0)
    def _():
        state_ref[...] = jnp.zeros_like(state_ref)

    dmask, qdec, kdec, cdec = dm_ref[...], qd_ref[...], kd_ref[...], cd_ref[...]

    def chunk(c, carry):
        rows = pl.ds(pl.multiple_of(c * CHUNK, CHUNK), CHUNK)
        q, k, v = q_ref[rows, :], k_ref[rows, :], v_ref[rows, :]
        st = state_ref[...]
        inner = _mm_nt(q, k) * dmask
        o = _mm(inner.astype(BF16), v) + _mm(q, st.astype(BF16)) * qdec
        kd_t = (k.astype(F32) * kdec).T.astype(BF16)
        state_ref[...] = st * cdec + _mm(kd_t, v)
        mu = jnp.mean(o, axis=-1, keepdims=True)
        oc = o - mu
        var = jnp.mean(oc * oc, axis=-1, keepdims=True)
        on = oc * lax.rsqrt(var + GN_EPS)
        o_ref[rows, :] = (on * g_ref[rows, :].astype(F32)).astype(o_ref.dtype)
        return carry

    lax.fori_loop(0, n_chunks, chunk, 0)


def _ret_tables():
    log_gamma = jnp.log1p(-jnp.exp2(-5.0 - jnp.arange(N_RET_HEADS, dtype=F32)))
    idx = jnp.arange(CHUNK, dtype=F32)
    rel = idx[:, None] - idx[None, :]
    dmask = jnp.where(rel >= 0, jnp.exp(log_gamma[:, None, None] * jnp.maximum(rel, 0.0)), 0.0)
    qdec = jnp.exp(log_gamma[:, None] * (idx + 1.0)[None, :])
    kdec = jnp.exp(log_gamma[:, None] * (CHUNK - 1.0 - idx)[None, :])
    cdec = jnp.exp(log_gamma * CHUNK)
    qdec = jnp.broadcast_to(qdec[:, :, None], (N_RET_HEADS, CHUNK, RET_V_DIM))
    kdec = jnp.broadcast_to(kdec[:, :, None], (N_RET_HEADS, CHUNK, RET_QK_DIM))
    cdec = jnp.broadcast_to(cdec[:, None, None], (N_RET_HEADS, 1, RET_V_DIM))
    return dmask, qdec, kdec, cdec


def retention(q, k, v, g, tables, out_dtype, tc=1024):
    b, s, _ = q.shape
    n_chunks = tc // CHUNK
    qk_spec = pl.BlockSpec((None, tc, RET_QK_DIM), lambda bi, h, j: (bi, j, h))
    v_spec = pl.BlockSpec((None, tc, RET_V_DIM), lambda bi, h, j: (bi, j, h))

    def tab(shape):
        return pl.BlockSpec((None,) + shape, lambda bi, h, j: (h, 0, 0))

    return pl.pallas_call(
        functools.partial(_ret_kernel, n_chunks=n_chunks),
        grid=(b, N_RET_HEADS, s // tc),
        in_specs=[qk_spec, qk_spec, v_spec, v_spec,
                  tab((CHUNK, CHUNK)), tab((CHUNK, RET_V_DIM)), tab((CHUNK, RET_QK_DIM)),
                  tab((1, RET_V_DIM))],
        out_specs=v_spec,
        out_shape=jax.ShapeDtypeStruct((b, s, RET_V_WIDTH), out_dtype),
        scratch_shapes=[pltpu.VMEM((RET_QK_DIM, RET_V_DIM), F32)],
        compiler_params=_params("parallel", "parallel", "arbitrary"),
        name="retention",
    )(q, k, v, g, *tables)


def _merge_kernel(oa_ref, or_ref, pa_ref, pr_ref, ga_ref, gb_ref, o_ref, acc_ref):
    kk = pl.program_id(1)

    @pl.when(kk == 0)
    def _():
        acc_ref[...] = jnp.zeros_like(acc_ref)

    acc_ref[...] += _mm(or_ref[...], pr_ref[...])

    @pl.when(kk == pl.num_programs(1) - 1)
    def _():
        ya = _mm(oa_ref[...], pa_ref[...])
        merged = ga_ref[...].astype(F32) * ya + gb_ref[...].astype(F32) * acc_ref[...]
        o_ref[...] = merged.astype(o_ref.dtype)


def merge(o_a, o_r, p_a, p_r, gates, out_dtype, tm=512, tk=512):
    t, wa = o_a.shape
    wr = o_r.shape[1]
    d = p_a.shape[1]
    return pl.pallas_call(
        _merge_kernel,
        grid=(t // tm, wr // tk),
        in_specs=[pl.BlockSpec((tm, wa), lambda i, kk: (i, 0)),
                  pl.BlockSpec((tm, tk), lambda i, kk: (i, kk)),
                  pl.BlockSpec((wa, d), lambda i, kk: (0, 0)),
                  pl.BlockSpec((tk, d), lambda i, kk: (kk, 0)),
                  pl.BlockSpec((tm, d), lambda i, kk: (i, 0)),
                  pl.BlockSpec((tm, d), lambda i, kk: (i, 1))],
        out_specs=pl.BlockSpec((tm, d), lambda i, kk: (i, 0)),
        out_shape=jax.ShapeDtypeStruct((t, d), out_dtype),
        scratch_shapes=[pltpu.VMEM((tm, d), F32)],
        compiler_params=_params("parallel", "arbitrary"),
        name="gated_merge",
    )(o_a, o_r, p_a, p_r, gates, gates)


def _rope_tables(seq, dim):
    inv_freq = 1.0 / (ROPE_THETA ** (jnp.arange(0, dim, 2, dtype=F32) / dim))
    ang = jnp.arange(seq, dtype=F32)[:, None] * inv_freq[None, :]
    return jnp.cos(ang), jnp.sin(ang)


def kernel(x, ffn1_norm, ffn1_w_gate, ffn1_w_up, ffn1_w_down, mix_norm, w_in, w_proj_attn, w_proj_ret,
           w_out, ffn2_norm, ffn2_w_gate, ffn2_w_up, ffn2_w_down, final_norm):
    b, s, d = x.shape
    t = b * s
    depth = w_in.shape[0]
    cos_a, sin_a = _rope_tables(s, HEAD_DIM)
    tab_a = (jnp.concatenate([cos_a, cos_a], axis=1), jnp.concatenate([-sin_a, sin_a], axis=1))
    tab_r = _rope_tables(s, RET_QK_DIM)
    ret_tabs = _ret_tables()
    off = IN_OFFSETS
    bf = lambda w: w.astype(BF16)

    x = x.reshape(t, d)
    for layer in range(depth):
        x = ffn(x, ffn1_norm[layer], bf(ffn1_w_gate[layer]), bf(ffn1_w_up[layer]), bf(ffn1_w_down[layer]))
        h = rmsnorm(x, mix_norm[layer], BF16)
        w = bf(w_in[layer])
        pj = functools.partial(proj, h, w, seq=s)
        q_a = pj(off[0], ATTN_WIDTH, "rot128", F32, tables=tab_a, scale=HEAD_DIM ** -0.5)
        k_a = pj(off[1], ATTN_WIDTH, "rot128", F32, tables=tab_a)
        v_a = pj(off[2], ATTN_WIDTH, "none", F32)
        q_r = pj(off[3], RET_QK_WIDTH, "rot256", BF16, tables=tab_r)
        k_r = pj(off[4], RET_QK_WIDTH, "rot256", BF16, tables=tab_r, scale=RET_QK_DIM ** -0.5)
        v_r = pj(off[5], RET_V_WIDTH, "none", BF16)
        g_r = pj(off[6], RET_V_WIDTH, "silu", BF16)
        gates = pj(off[7], 2 * D_MODEL, "sigmoid", BF16)
        o_a = attention(q_a.reshape(b, s, -1), k_a.reshape(b, s, -1), v_a.reshape(b, s, -1), BF16)
        o_r = retention(q_r.reshape(b, s, -1), k_r.reshape(b, s, -1), v_r.reshape(b, s, -1),
                        g_r.reshape(b, s, -1), ret_tabs, BF16)
        merged = merge(o_a.reshape(t, -1), o_r.reshape(t, -1), bf(w_proj_attn[layer]), bf(w_proj_ret[layer]),
                       gates, BF16)
        x = proj(merged, bf(w_out[layer]), 0, D_MODEL, "residual", F32, resid=x)
        x = ffn(x, ffn2_norm[layer], bf(ffn2_w_gate[layer]), bf(ffn2_w_up[layer]), bf(ffn2_w_down[layer]))
    out = rmsnorm(x, final_norm, F32)
    return out.reshape(b, s, d)
```

```python
import functools

import jax
import jax.numpy as jnp
import numpy as np
from jax import lax
from jax.experimental import pallas as pl
from jax.experimental.pallas import tpu as pltpu

F32 = jnp.float32
BF16 = jnp.bfloat16

LANES = 128
VMEM_LIMIT_BYTES = 56 * 1024 * 1024

D_MODEL = 2048
HEAD_DIM = 128
ATTN_GROUPS = ((128, 1), (512, 4), (2048, 16))
ATTN_HEADS_PER_GROUP = 4
N_ATTN_HEADS = len(ATTN_GROUPS) * ATTN_HEADS_PER_GROUP
ATTN_WIDTH = N_ATTN_HEADS * HEAD_DIM
ATTN_OUT_WIDTH = ATTN_HEADS_PER_GROUP * HEAD_DIM
BLOCK = 128
ROPE_THETA = 10000.0
NEG_INF = -1e30
N_RET_HEADS = D_MODEL // 256
RET_QK_DIM = 256
RET_V_DIM = 512
RET_QK_WIDTH = N_RET_HEADS * RET_QK_DIM
RET_V_WIDTH = N_RET_HEADS * RET_V_DIM
CHUNK = 128
GN_EPS = 1e-5
D_FF = 5632
RMS_EPS = 1e-6
IN_SPLITS = (ATTN_WIDTH, ATTN_WIDTH, ATTN_WIDTH, RET_QK_WIDTH, RET_QK_WIDTH,
             RET_V_WIDTH, RET_V_WIDTH, D_MODEL, D_MODEL)
IN_OFFSETS = tuple(int(o) for o in np.cumsum((0,) + IN_SPLITS[:-1]))

ATTN_SUPER = BLOCK * ATTN_GROUPS[-1][1]
ATTN_UNROLL = 8
RET_CHUNK = 256
RET_SLAB = 32


def _params(*sem):
    return pltpu.CompilerParams(dimension_semantics=sem, vmem_limit_bytes=VMEM_LIMIT_BYTES)


def _mm(a, b):
    return jnp.dot(a, b, preferred_element_type=F32)


def _mm_nt(a, b):
    return lax.dot_general(a, b, (((1,), (1,)), ((), ())), preferred_element_type=F32)


def _rms(x, g):
    return x * lax.rsqrt(jnp.mean(x * x, axis=-1, keepdims=True) + RMS_EPS) * g


def _ffn_kernel(x_ref, g_ref, wg_ref, wu_ref, wd_ref, *rest, emit_x, emit_norm):
    if emit_norm:
        g2_ref, rest = rest[0], rest[1:]
    outs, (h_ref, acc_ref) = rest[:-2], rest[-2:]
    f = pl.program_id(1)

    @pl.when(f == 0)
    def _():
        h_ref[...] = _rms(x_ref[...], g_ref[...]).astype(BF16)
        acc_ref[...] = jnp.zeros_like(acc_ref)

    h = h_ref[...]
    gate = _mm(h, wg_ref[...])
    up = _mm(h, wu_ref[...])
    act = gate * jax.nn.sigmoid(gate) * up
    acc_ref[...] += _mm(act.astype(BF16), wd_ref[...])

    @pl.when(f == pl.num_programs(1) - 1)
    def _():
        y = x_ref[...] + 0.5 * acc_ref[...]
        if emit_x:
            outs[0][...] = y
        if emit_norm:
            outs[-1][...] = _rms(y, g2_ref[...]).astype(outs[-1].dtype)


def ffn(x, g, w_gate, w_up, w_down, post_gain=None, post_dtype=None, emit_x=True, tm=512, tf=512):
    t, d = x.shape
    dff = w_gate.shape[1]
    emit_norm = post_gain is not None
    row_spec = pl.BlockSpec((tm, d), lambda i, f: (i, 0))
    vec_spec = pl.BlockSpec((1, d), lambda i, f: (0, 0))
    in_specs = [row_spec, vec_spec,
                pl.BlockSpec((d, tf), lambda i, f: (0, f)),
                pl.BlockSpec((d, tf), lambda i, f: (0, f)),
                pl.BlockSpec((tf, d), lambda i, f: (f, 0))]
    args = [x, g.reshape(1, d), w_gate, w_up, w_down]
    out_specs, out_shape = [], []
    if emit_x:
        out_specs.append(row_spec)
        out_shape.append(jax.ShapeDtypeStruct((t, d), F32))
    if emit_norm:
        in_specs.append(vec_spec)
        args.append(post_gain.reshape(1, d))
        out_specs.append(row_spec)
        out_shape.append(jax.ShapeDtypeStruct((t, d), post_dtype))
    res = pl.pallas_call(
        functools.partial(_ffn_kernel, emit_x=emit_x, emit_norm=emit_norm),
        grid=(t // tm, dff // tf),
        in_specs=in_specs,
        out_specs=out_specs,
        out_shape=out_shape,
        scratch_shapes=[pltpu.VMEM((tm, d), BF16), pltpu.VMEM((tm, d), F32)],
        compiler_params=_params("parallel", "arbitrary"),
        name="ffn",
    )(*args)
    return res if len(res) > 1 else res[0]


def _proj_kernel(*refs, mode, scale, tn):
    if mode in ("rot128", "rot256"):
        a_ref, w_ref, c_ref, s_ref, o_ref = refs
    elif mode == "residual":
        a_ref, w_ref, x_ref, o_ref = refs
    else:
        a_ref, w_ref, o_ref = refs
    y = _mm(a_ref[...], w_ref[...])
    if scale is not None:
        y = y * scale
    if mode == "rot128":
        c, s = c_ref[...], s_ref[...]
        for b in range(tn // LANES):
            yb = y[:, b * LANES:(b + 1) * LANES]
            ob = yb * c + pltpu.roll(yb, LANES // 2, axis=1) * s
            o_ref[:, b * LANES:(b + 1) * LANES] = ob.astype(o_ref.dtype)
    elif mode == "rot256":
        c, s = c_ref[...], s_ref[...]
        for b in range(tn // (2 * LANES)):
            y1 = y[:, (2 * b) * LANES:(2 * b + 1) * LANES]
            y2 = y[:, (2 * b + 1) * LANES:(2 * b + 2) * LANES]
            o_ref[:, (2 * b) * LANES:(2 * b + 1) * LANES] = (y1 * c - y2 * s).astype(o_ref.dtype)
            o_ref[:, (2 * b + 1) * LANES:(2 * b + 2) * LANES] = (y2 * c + y1 * s).astype(o_ref.dtype)
    elif mode == "silu":
        o_ref[...] = (y * jax.nn.sigmoid(y)).astype(o_ref.dtype)
    elif mode == "sigmoid":
        o_ref[...] = jax.nn.sigmoid(y).astype(o_ref.dtype)
    elif mode == "residual":
        o_ref[...] = (x_ref[...] + y).astype(o_ref.dtype)
    else:
        o_ref[...] = y.astype(o_ref.dtype)


def proj(a, w, mode, out_dtype, *, tables=None, scale=None, resid=None, seq=None, tm=1024, tn=512):
    t, k = a.shape
    n_cols = w.shape[1]
    in_specs = [pl.BlockSpec((tm, k), lambda i, j: (i, 0)),
                pl.BlockSpec((k, tn), lambda i, j: (0, j))]
    args = [a, w]
    if mode in ("rot128", "rot256"):
        nsb = seq // tm
        tab_spec = pl.BlockSpec((tm, LANES), lambda i, j: (i % nsb, 0))
        in_specs += [tab_spec, tab_spec]
        args += list(tables)
    elif mode == "residual":
        in_specs.append(pl.BlockSpec((tm, tn), lambda i, j: (i, j)))
        args.append(resid)
    return pl.pallas_call(
        functools.partial(_proj_kernel, mode=mode, scale=scale, tn=tn),
        grid=(t // tm, n_cols // tn),
        in_specs=in_specs,
        out_specs=pl.BlockSpec((tm, tn), lambda i, j: (i, j)),
        out_shape=jax.ShapeDtypeStruct((t, n_cols), out_dtype),
        compiler_params=_params("parallel", "arbitrary"),
        name="proj_" + mode,
    )(*args)


def _attn_kernel(q0, q1, q2, k0, k1, k2, v0, v1, v2, kp0, kp1, kp2, vp0, vp1, vp2,
                 o_ref, a0, a1, a2, m0, m1, m2, l0, l1, l2):
    row = lax.broadcasted_iota(jnp.int32, (BLOCK, 2 * BLOCK), 0)
    col = lax.broadcasted_iota(jnp.int32, (BLOCK, 2 * BLOCK), 1)
    band = jnp.logical_and(col >= row, col <= row + BLOCK)
    band_halo = jnp.logical_and(band, col >= jnp.where(pl.program_id(2) > 0, 0, BLOCK))
    ones = jnp.ones((2 * BLOCK, LANES), BF16)

    def tiles(items, a_ref, m_ref, l_ref):
        s = [jnp.where(mask, _mm_nt(q.astype(BF16), jnp.concatenate([kp, kc], axis=0).astype(BF16)), NEG_INF)
             for _, q, kp, kc, _, _, mask in items]
        m = [jnp.max(jnp.maximum(si[:, :BLOCK], si[:, BLOCK:]), axis=1, keepdims=True) for si in s]
        p = [jnp.where(it[6], jnp.exp(si - mi), 0.0).astype(BF16) for it, si, mi in zip(items, s, m)]
        al = [_mm(pi, jnp.concatenate([jnp.concatenate([vp, vc], axis=0).astype(BF16), ones], axis=1))
              for (_, _, _, _, vp, vc, _), pi in zip(items, p)]
        for (rows, *_), ali, mi in zip(items, al, m):
            a_ref[rows, :] = ali[:, :LANES]
            l_ref[rows, :] = ali[:, LANES:]
            m_ref[rows, :] = jnp.broadcast_to(mi, (BLOCK, LANES))

    def group(q, k, v, kp, vp, a_ref, m_ref, l_ref, d):
        span = BLOCK * d
        n_sb = ATTN_SUPER // span
        u = ATTN_UNROLL

        def rows_at(start):
            return pl.ds(start, BLOCK, stride=d) if d > 1 else pl.ds(start, BLOCK)

        def item(start, halo):
            rows = rows_at(start)
            if halo:
                return (rows, q[rows, :], kp[rows, :], k[rows, :], vp[rows, :], v[rows, :], band_halo)
            prev = rows_at(start - span)
            return (rows, q[rows, :], k[prev, :], k[rows, :], v[prev, :], v[rows, :], band)

        def batch(items):
            tiles(items, a_ref, m_ref, l_ref)

        if d >= u:
            per = d // u

            def halo_batch(i, carry):
                batch([item(i * u + j, True) for j in range(u)])
                return carry

            lax.fori_loop(0, per, halo_batch, 0)

            def later(sb, carry):
                for rb in range(per):
                    batch([item(sb * span + rb * u + j, False) for j in range(u)])
                return carry

            if n_sb > 1:
                lax.fori_loop(1, n_sb, later, 0)
        else:
            spb = u // d
            batch([item((j // d) * span + j % d, j < d) for j in range(u)])

            def later(i, carry):
                base = pl.multiple_of(i * (spb * span), spb * span)
                batch([item(base + (j // d) * span + j % d, False) for j in range(u)])
                return carry

            if n_sb > spb:
                lax.fori_loop(1, n_sb // spb, later, 0)

    group(q0, k0, v0, kp0, vp0, a0, m0, l0, ATTN_GROUPS[0][1])
    group(q1, k1, v1, kp1, vp1, a1, m1, l1, ATTN_GROUPS[1][1])
    group(q2, k2, v2, kp2, vp2, a2, m2, l2, ATTN_GROUPS[2][1])

    rc = 256

    def merge(i, carry):
        rows = pl.ds(pl.multiple_of(i * rc, rc), rc)
        ma, mb, mc = m0[rows, :], m1[rows, :], m2[rows, :]
        mx = jnp.maximum(jnp.maximum(ma, mb), mc)
        ea, eb, ec = jnp.exp(ma - mx), jnp.exp(mb - mx), jnp.exp(mc - mx)
        num = ea * a0[rows, :] + eb * a1[rows, :] + ec * a2[rows, :]
        den = ea * l0[rows, :] + eb * l1[rows, :] + ec * l2[rows, :]
        o_ref[rows, :] = (num / den).astype(o_ref.dtype)
        return carry

    lax.fori_loop(0, ATTN_SUPER // rc, merge, 0)


def attention(q, k, v, out_dtype):
    b, s, _ = q.shape
    hg = ATTN_HEADS_PER_GROUP
    nsb = s // ATTN_SUPER

    def cur(g):
        return pl.BlockSpec((None, ATTN_SUPER, HEAD_DIM), lambda bi, h, j: (bi, j, g * hg + h))

    def prev(g):
        span = BLOCK * ATTN_GROUPS[g][1]
        per = ATTN_SUPER // span
        return pl.BlockSpec((None, span, HEAD_DIM),
                            lambda bi, h, j: (bi, jnp.maximum(j * per - 1, 0), g * hg + h))

    in_specs = ([cur(g) for g in range(3)] * 3 + [prev(g) for g in range(3)] * 2)
    scratch = [pltpu.VMEM((ATTN_SUPER, LANES), F32) for _ in range(9)]
    return pl.pallas_call(
        _attn_kernel,
        grid=(b, hg, nsb),
        in_specs=in_specs,
        out_specs=pl.BlockSpec((None, ATTN_SUPER, HEAD_DIM), lambda bi, h, j: (bi, j, h)),
        out_shape=jax.ShapeDtypeStruct((b, s, ATTN_OUT_WIDTH), out_dtype),
        scratch_shapes=scratch,
        compiler_params=_params("parallel", "parallel", "arbitrary"),
        name="dilated_attn",
    )(q, q, q, k, k, k, v, v, v, k, k, k, v, v, v)


def _ret_kernel(q_ref, k_ref, v_ref, g_ref, dm_ref, qd_ref, kd_ref, cd_ref, o_ref,
                state_ref, kv_ref, sbf_ref, *, n_chunks):
    @pl.when(pl.program_id(2) == 0)
    def _():
        state_ref[...] = jnp.zeros_like(state_ref)

    c_len = RET_CHUNK
    rows = [pl.ds(c * c_len, c_len) for c in range(n_chunks)]

    dmask, qdec, kdec, cdec = dm_ref[...], qd_ref[...], kd_ref[...], cd_ref[...]
    kd_t = [(k_ref[r, :].astype(F32) * kdec).T.astype(BF16) for r in rows]
    for c, r in enumerate(rows):
        kv_ref[c] = _mm(kd_t[c], v_ref[r, :])
    scores = [_mm_nt(q_ref[r, :], k_ref[r, :]) for r in rows]

    for i in range(RET_QK_DIM // RET_SLAB):
        rs = pl.ds(i * RET_SLAB, RET_SLAB)
        st = state_ref[rs, :]
        for c in range(n_chunks):
            sbf_ref[c, rs, :] = st.astype(BF16)
            st = st * cdec + kv_ref[c, rs, :]
        state_ref[rs, :] = st

    for c, r in enumerate(rows):
        lhs = jnp.concatenate([(scores[c] * dmask).astype(BF16),
                               (q_ref[r, :].astype(F32) * qdec).astype(BF16)], axis=1)
        rhs = jnp.concatenate([v_ref[r, :], sbf_ref[c]], axis=0)
        o = _mm(lhs, rhs)
        mu = jnp.mean(o, axis=-1, keepdims=True)
        oc = o - mu
        var = jnp.mean(oc * oc, axis=-1, keepdims=True)
        on = oc * lax.rsqrt(var + GN_EPS)
        o_ref[r, :] = (on * g_ref[r, :].astype(F32)).astype(o_ref.dtype)


def _ret_tables():
    c_len = RET_CHUNK
    log_gamma = jnp.log1p(-jnp.exp2(-5.0 - jnp.arange(N_RET_HEADS, dtype=F32)))
    idx = jnp.arange(c_len, dtype=F32)
    rel = idx[:, None] - idx[None, :]
    dmask = jnp.where(rel >= 0, jnp.exp(log_gamma[:, None, None] * jnp.maximum(rel, 0.0)), 0.0)
    qdec = jnp.exp(log_gamma[:, None] * (idx + 1.0)[None, :])
    kdec = jnp.exp(log_gamma[:, None] * (c_len - 1.0 - idx)[None, :])
    cdec = jnp.exp(log_gamma * c_len)
    qdec = jnp.broadcast_to(qdec[:, :, None], (N_RET_HEADS, c_len, RET_QK_DIM))
    kdec = jnp.broadcast_to(kdec[:, :, None], (N_RET_HEADS, c_len, RET_QK_DIM))
    cdec = jnp.broadcast_to(cdec[:, None, None], (N_RET_HEADS, 1, RET_V_DIM))
    return dmask, qdec, kdec, cdec


def retention(q, k, v, g, tables, out_dtype, tc=1024):
    b, s, _ = q.shape
    n_chunks = tc // RET_CHUNK
    qk_spec = pl.BlockSpec((None, tc, RET_QK_DIM), lambda bi, h, j: (bi, j, h))
    v_spec = pl.BlockSpec((None, tc, RET_V_DIM), lambda bi, h, j: (bi, j, h))

    def tab(shape):
        return pl.BlockSpec((None,) + shape, lambda bi, h, j: (h, 0, 0))

    return pl.pallas_call(
        functools.partial(_ret_kernel, n_chunks=n_chunks),
        grid=(b, N_RET_HEADS, s // tc),
        in_specs=[qk_spec, qk_spec, v_spec, v_spec,
                  tab((RET_CHUNK, RET_CHUNK)), tab((RET_CHUNK, RET_QK_DIM)), tab((RET_CHUNK, RET_QK_DIM)),
                  tab((1, RET_V_DIM))],
        out_specs=v_spec,
        out_shape=jax.ShapeDtypeStruct((b, s, RET_V_WIDTH), out_dtype),
        scratch_shapes=[pltpu.VMEM((RET_QK_DIM, RET_V_DIM), F32),
                        pltpu.VMEM((n_chunks, RET_QK_DIM, RET_V_DIM), F32),
                        pltpu.VMEM((n_chunks, RET_QK_DIM, RET_V_DIM), BF16)],
        compiler_params=_params("parallel", "parallel", "arbitrary"),
        name="retention",
    )(q, k, v, g, *tables)


def _merge_kernel(oa_ref, or_ref, pa_ref, pr_ref, ga_ref, gb_ref, o_ref):
    ya = _mm(oa_ref[...], pa_ref[...])
    yr = _mm(or_ref[...], pr_ref[...])
    merged = ga_ref[...].astype(F32) * ya + gb_ref[...].astype(F32) * yr
    o_ref[...] = merged.astype(o_ref.dtype)


def merge(o_a, o_r, p_a, p_r, gates, out_dtype, tm=1024, tn=512):
    t, wa = o_a.shape
    wr = o_r.shape[1]
    d = p_a.shape[1]
    nb = d // tn
    return pl.pallas_call(
        _merge_kernel,
        grid=(t // tm, nb),
        in_specs=[pl.BlockSpec((tm, wa), lambda i, j: (i, 0)),
                  pl.BlockSpec((tm, wr), lambda i, j: (i, 0)),
                  pl.BlockSpec((wa, tn), lambda i, j: (0, j)),
                  pl.BlockSpec((wr, tn), lambda i, j: (0, j)),
                  pl.BlockSpec((tm, tn), lambda i, j: (i, j)),
                  pl.BlockSpec((tm, tn), lambda i, j: (i, nb + j))],
        out_specs=pl.BlockSpec((tm, tn), lambda i, j: (i, j)),
        out_shape=jax.ShapeDtypeStruct((t, d), out_dtype),
        compiler_params=_params("parallel", "arbitrary"),
        name="gated_merge",
    )(o_a, o_r, p_a, p_r, gates, gates)


def _rope_tables(seq, dim):
    inv_freq = 1.0 / (ROPE_THETA ** (jnp.arange(0, dim, 2, dtype=F32) / dim))
    ang = jnp.arange(seq, dtype=F32)[:, None] * inv_freq[None, :]
    return jnp.cos(ang), jnp.sin(ang)


def kernel(x, ffn1_norm, ffn1_w_gate, ffn1_w_up, ffn1_w_down, mix_norm, w_in, w_proj_attn, w_proj_ret,
           w_out, ffn2_norm, ffn2_w_gate, ffn2_w_up, ffn2_w_down, final_norm):
    b, s, d = x.shape
    t = b * s
    depth = w_in.shape[0]
    cos_a, sin_a = _rope_tables(s, HEAD_DIM)
    tab_a = (jnp.concatenate([cos_a, cos_a], axis=1), jnp.concatenate([-sin_a, sin_a], axis=1))
    tab_r = _rope_tables(s, RET_QK_DIM)
    ret_tabs = _ret_tables()
    off = IN_OFFSETS
    bf = lambda w: w.astype(BF16)

    x = x.reshape(t, d)
    for layer in range(depth):
        x, h = ffn(x, ffn1_norm[layer], bf(ffn1_w_gate[layer]), bf(ffn1_w_up[layer]), bf(ffn1_w_down[layer]),
                   post_gain=mix_norm[layer], post_dtype=BF16)
        def pj(seg, width, mode, dtype, **kw):
            w = bf(w_in[layer, :, off[seg]:off[seg] + width])
            return proj(h, w, mode, dtype, seq=s, **kw)

        wide, narrow = 1024, 768
        q_a = pj(0, ATTN_WIDTH, "rot128", F32, tables=tab_a, scale=HEAD_DIM ** -0.5, tn=narrow)
        k_a = pj(1, ATTN_WIDTH, "rot128", F32, tables=tab_a, tn=narrow)
        v_a = pj(2, ATTN_WIDTH, "none", F32, tn=narrow)
        q_r = pj(3, RET_QK_WIDTH, "rot256", BF16, tables=tab_r, tn=wide)
        k_r = pj(4, RET_QK_WIDTH, "rot256", BF16, tables=tab_r, scale=RET_QK_DIM ** -0.5, tn=wide)
        v_r = pj(5, RET_V_WIDTH, "none", BF16, tn=wide)
        g_r = pj(6, RET_V_WIDTH, "silu", BF16, tn=wide)
        gates = pj(7, 2 * D_MODEL, "sigmoid", BF16, tn=wide)
        o_a = attention(q_a.reshape(b, s, -1), k_a.reshape(b, s, -1), v_a.reshape(b, s, -1), BF16)
        o_r = retention(q_r.reshape(b, s, -1), k_r.reshape(b, s, -1), v_r.reshape(b, s, -1),
                        g_r.reshape(b, s, -1), ret_tabs, BF16)
        merged = merge(o_a.reshape(t, -1), o_r.reshape(t, -1), bf(w_proj_attn[layer]), bf(w_proj_ret[layer]),
                       gates, BF16)
        x = proj(merged, bf(w_out[layer]), "residual", F32, resid=x, tn=1024)
        last = layer == depth - 1
        x = ffn(x, ffn2_norm[layer], bf(ffn2_w_gate[layer]), bf(ffn2_w_up[layer]), bf(ffn2_w_down[layer]),
                post_gain=final_norm if last else None, post_dtype=F32, emit_x=not last)
    return x.reshape(b, s, d)
```

```python
import functools

import jax
import jax.numpy as jnp
import numpy as np
from jax import lax
from jax.experimental import pallas as pl
from jax.experimental.pallas import tpu as pltpu

F32 = jnp.float32
BF16 = jnp.bfloat16

LANES = 128
VMEM_LIMIT_BYTES = 56 * 1024 * 1024

D_MODEL = 2048
HEAD_DIM = 128
ATTN_GROUPS = ((128, 1), (512, 4), (2048, 16))
ATTN_HEADS_PER_GROUP = 4
N_ATTN_HEADS = len(ATTN_GROUPS) * ATTN_HEADS_PER_GROUP
ATTN_WIDTH = N_ATTN_HEADS * HEAD_DIM
ATTN_OUT_WIDTH = ATTN_HEADS_PER_GROUP * HEAD_DIM
BLOCK = 128
ROPE_THETA = 10000.0
NEG_INF = -1e30
N_RET_HEADS = D_MODEL // 256
RET_QK_DIM = 256
RET_V_DIM = 512
RET_QK_WIDTH = N_RET_HEADS * RET_QK_DIM
RET_V_WIDTH = N_RET_HEADS * RET_V_DIM
CHUNK = 128
GN_EPS = 1e-5
D_FF = 5632
RMS_EPS = 1e-6
IN_SPLITS = (ATTN_WIDTH, ATTN_WIDTH, ATTN_WIDTH, RET_QK_WIDTH, RET_QK_WIDTH,
             RET_V_WIDTH, RET_V_WIDTH, D_MODEL, D_MODEL)
IN_OFFSETS = tuple(int(o) for o in np.cumsum((0,) + IN_SPLITS[:-1]))

ATTN_SUPER = BLOCK * ATTN_GROUPS[-1][1]
ATTN_UNROLL = 8
RET_CHUNK = 256
RET_SLAB = 32


def _params(*sem):
    return pltpu.CompilerParams(dimension_semantics=sem, vmem_limit_bytes=VMEM_LIMIT_BYTES)


def _mm(a, b):
    return jnp.dot(a, b, preferred_element_type=F32)


def _sigmoid(x):
    return 0.5 * jnp.tanh(0.5 * x) + 0.5


def _mm_nt(a, b):
    return lax.dot_general(a, b, (((1,), (1,)), ((), ())), preferred_element_type=F32)


def _rms(x, g):
    return x * lax.rsqrt(jnp.mean(x * x, axis=-1, keepdims=True) + RMS_EPS) * g


def _ffn_kernel(x_ref, g_ref, wg_ref, wu_ref, wd_ref, *rest, emit_x, emit_norm):
    if emit_norm:
        g2_ref, rest = rest[0], rest[1:]
    outs, (h_ref, acc_ref) = rest[:-2], rest[-2:]
    f = pl.program_id(1)

    @pl.when(f == 0)
    def _():
        h_ref[...] = _rms(x_ref[...], g_ref[...]).astype(BF16)
        acc_ref[...] = jnp.zeros_like(acc_ref)

    h = h_ref[...]
    gate = _mm(h, wg_ref[...])
    up = _mm(h, wu_ref[...])
    act = gate * jax.nn.sigmoid(gate) * up
    acc_ref[...] += _mm(act.astype(BF16), wd_ref[...])

    @pl.when(f == pl.num_programs(1) - 1)
    def _():
        y = x_ref[...] + 0.5 * acc_ref[...]
        if emit_x:
            outs[0][...] = y
        if emit_norm:
            outs[-1][...] = _rms(y, g2_ref[...]).astype(outs[-1].dtype)


def ffn(x, g, w_gate, w_up, w_down, layer, post_gain=None, post_dtype=None, emit_x=True, tm=512, tf=512):
    t, d = x.shape
    dff = w_gate.shape[2]
    emit_norm = post_gain is not None
    row_spec = pl.BlockSpec((tm, d), lambda i, f: (i, 0))
    vec_spec = pl.BlockSpec((1, d), lambda i, f: (0, 0))
    in_specs = [row_spec, vec_spec,
                pl.BlockSpec((None, d, tf), lambda i, f: (layer, 0, f)),
                pl.BlockSpec((None, d, tf), lambda i, f: (layer, 0, f)),
                pl.BlockSpec((None, tf, d), lambda i, f: (layer, f, 0))]
    args = [x, g.reshape(1, d), w_gate, w_up, w_down]
    out_specs, out_shape = [], []
    if emit_x:
        out_specs.append(row_spec)
        out_shape.append(jax.ShapeDtypeStruct((t, d), F32))
    if emit_norm:
        in_specs.append(vec_spec)
        args.append(post_gain.reshape(1, d))
        out_specs.append(row_spec)
        out_shape.append(jax.ShapeDtypeStruct((t, d), post_dtype))
    res = pl.pallas_call(
        functools.partial(_ffn_kernel, emit_x=emit_x, emit_norm=emit_norm),
        grid=(t // tm, dff // tf),
        in_specs=in_specs,
        out_specs=out_specs,
        out_shape=out_shape,
        scratch_shapes=[pltpu.VMEM((tm, d), BF16), pltpu.VMEM((tm, d), F32)],
        compiler_params=_params("parallel", "arbitrary"),
        name="ffn",
    )(*args)
    return res if len(res) > 1 else res[0]


def _proj_kernel(*refs, mode, scale, tn):
    if mode in ("rot128", "rot256"):
        a_ref, w_ref, c_ref, s_ref, o_ref = refs
    elif mode == "residual":
        a_ref, w_ref, x_ref, o_ref = refs
    else:
        a_ref, w_ref, o_ref = refs
    y = _mm(a_ref[...], w_ref[...].astype(BF16))
    if scale is not None:
        y = y * scale
    if mode == "rot128":
        c, s = c_ref[...], s_ref[...]
        for b in range(tn // LANES):
            yb = y[:, b * LANES:(b + 1) * LANES]
            ob = yb * c + pltpu.roll(yb, LANES // 2, axis=1) * s
            o_ref[:, b * LANES:(b + 1) * LANES] = ob.astype(o_ref.dtype)
    elif mode == "rot256":
        c, s = c_ref[...], s_ref[...]
        for b in range(tn // (2 * LANES)):
            y1 = y[:, (2 * b) * LANES:(2 * b + 1) * LANES]
            y2 = y[:, (2 * b + 1) * LANES:(2 * b + 2) * LANES]
            o_ref[:, (2 * b) * LANES:(2 * b + 1) * LANES] = (y1 * c - y2 * s).astype(o_ref.dtype)
            o_ref[:, (2 * b + 1) * LANES:(2 * b + 2) * LANES] = (y2 * c + y1 * s).astype(o_ref.dtype)
    elif mode == "silu":
        o_ref[...] = (y * _sigmoid(y)).astype(o_ref.dtype)
    elif mode == "sigmoid":
        o_ref[...] = _sigmoid(y).astype(o_ref.dtype)
    elif mode == "residual":
        o_ref[...] = (x_ref[...] + y).astype(o_ref.dtype)
    else:
        o_ref[...] = y.astype(o_ref.dtype)


def proj(a, w, layer, col_off, n_cols, mode, out_dtype, *, tables=None, scale=None, resid=None, seq=None,
         tm=2048, tn=512):
    t, k = a.shape
    jb = col_off // tn
    assert col_off % tn == 0 and n_cols % tn == 0
    in_specs = [pl.BlockSpec((tm, k), lambda i, j: (i, 0)),
                pl.BlockSpec((None, k, tn), lambda i, j: (layer, 0, jb + j))]
    args = [a, w]
    if mode in ("rot128", "rot256"):
        nsb = seq // tm
        tab_spec = pl.BlockSpec((tm, LANES), lambda i, j: (i % nsb, 0))
        in_specs += [tab_spec, tab_spec]
        args += list(tables)
    elif mode == "residual":
        in_specs.append(pl.BlockSpec((tm, tn), lambda i, j: (i, j)))
        args.append(resid)
    return pl.pallas_call(
        functools.partial(_proj_kernel, mode=mode, scale=scale, tn=tn),
        grid=(t // tm, n_cols // tn),
        in_specs=in_specs,
        out_specs=pl.BlockSpec((tm, tn), lambda i, j: (i, j)),
        out_shape=jax.ShapeDtypeStruct((t, n_cols), out_dtype),
        compiler_params=_params("parallel", "arbitrary"),
        name="proj_" + mode,
    )(*args)


def _attn_kernel(q0, q1, q2, k0, k1, k2, v0, v1, v2, kp0, kp1, kp2, vp0, vp1, vp2,
                 o_ref, a0, a1, a2, m0, m1, m2, l0, l1, l2):
    row = lax.broadcasted_iota(jnp.int32, (BLOCK, 2 * BLOCK), 0)
    col = lax.broadcasted_iota(jnp.int32, (BLOCK, 2 * BLOCK), 1)
    band = jnp.logical_and(col >= row, col <= row + BLOCK)
    band_halo = jnp.logical_and(band, col >= jnp.where(pl.program_id(2) > 0, 0, BLOCK))
    ones = jnp.ones((2 * BLOCK, LANES), BF16)

    def tiles(items, a_ref, m_ref, l_ref):
        s = [jnp.where(mask, _mm_nt(q.astype(BF16), jnp.concatenate([kp, kc], axis=0).astype(BF16)), NEG_INF)
             for _, q, kp, kc, _, _, mask in items]
        m = [jnp.max(jnp.maximum(si[:, :BLOCK], si[:, BLOCK:]), axis=1, keepdims=True) for si in s]
        p = [jnp.where(it[6], jnp.exp(si - mi), 0.0).astype(BF16) for it, si, mi in zip(items, s, m)]
        al = [_mm(pi, jnp.concatenate([jnp.concatenate([vp, vc], axis=0).astype(BF16), ones], axis=1))
              for (_, _, _, _, vp, vc, _), pi in zip(items, p)]
        for (rows, *_), ali, mi in zip(items, al, m):
            a_ref[rows, :] = ali[:, :LANES]
            l_ref[rows, :] = ali[:, LANES:]
            m_ref[rows, :] = jnp.broadcast_to(mi, (BLOCK, LANES))

    def group(q, k, v, kp, vp, a_ref, m_ref, l_ref, d):
        span = BLOCK * d
        n_sb = ATTN_SUPER // span
        u = ATTN_UNROLL

        def rows_at(start):
            return pl.ds(start, BLOCK, stride=d) if d > 1 else pl.ds(start, BLOCK)

        def item(start, halo):
            rows = rows_at(start)
            if halo:
                return (rows, q[rows, :], kp[rows, :], k[rows, :], vp[rows, :], v[rows, :], band_halo)
            prev = rows_at(start - span)
            return (rows, q[rows, :], k[prev, :], k[rows, :], v[prev, :], v[rows, :], band)

        def batch(items):
            tiles(items, a_ref, m_ref, l_ref)

        if d >= u:
            per = d // u

            def halo_batch(i, carry):
                batch([item(i * u + j, True) for j in range(u)])
                return carry

            lax.fori_loop(0, per, halo_batch, 0)

            def later(sb, carry):
                for rb in range(per):
                    batch([item(sb * span + rb * u + j, False) for j in range(u)])
                return carry

            if n_sb > 1:
                lax.fori_loop(1, n_sb, later, 0)
        else:
            spb = u // d
            batch([item((j // d) * span + j % d, j < d) for j in range(u)])

            def later(i, carry):
                base = pl.multiple_of(i * (spb * span), spb * span)
                batch([item(base + (j // d) * span + j % d, False) for j in range(u)])
                return carry

            if n_sb > spb:
                lax.fori_loop(1, n_sb // spb, later, 0)

    group(q0, k0, v0, kp0, vp0, a0, m0, l0, ATTN_GROUPS[0][1])
    group(q1, k1, v1, kp1, vp1, a1, m1, l1, ATTN_GROUPS[1][1])
    group(q2, k2, v2, kp2, vp2, a2, m2, l2, ATTN_GROUPS[2][1])

    rc = 256

    def merge(i, carry):
        rows = pl.ds(pl.multiple_of(i * rc, rc), rc)
        ma, mb, mc = m0[rows, :], m1[rows, :], m2[rows, :]
        mx = jnp.maximum(jnp.maximum(ma, mb), mc)
        ea, eb, ec = jnp.exp(ma - mx), jnp.exp(mb - mx), jnp.exp(mc - mx)
        num = ea * a0[rows, :] + eb * a1[rows, :] + ec * a2[rows, :]
        den = ea * l0[rows, :] + eb * l1[rows, :] + ec * l2[rows, :]
        o_ref[rows, :] = (num / den).astype(o_ref.dtype)
        return carry

    lax.fori_loop(0, ATTN_SUPER // rc, merge, 0)


def attention(q, k, v, out_dtype):
    b, s, _ = q.shape
    hg = ATTN_HEADS_PER_GROUP
    nsb = s // ATTN_SUPER

    def cur(g):
        return pl.BlockSpec((None, ATTN_SUPER, HEAD_DIM), lambda bi, h, j: (bi, j, g * hg + h))

    def prev(g):
        span = BLOCK * ATTN_GROUPS[g][1]
        per = ATTN_SUPER // span
        return pl.BlockSpec((None, span, HEAD_DIM),
                            lambda bi, h, j: (bi, jnp.maximum(j * per - 1, 0), g * hg + h))

    in_specs = ([cur(g) for g in range(3)] * 3 + [prev(g) for g in range(3)] * 2)
    scratch = [pltpu.VMEM((ATTN_SUPER, LANES), F32) for _ in range(9)]
    return pl.pallas_call(
        _attn_kernel,
        grid=(b, hg, nsb),
        in_specs=in_specs,
        out_specs=pl.BlockSpec((None, ATTN_SUPER, HEAD_DIM), lambda bi, h, j: (bi, j, h)),
        out_shape=jax.ShapeDtypeStruct((b, s, ATTN_OUT_WIDTH), out_dtype),
        scratch_shapes=scratch,
        compiler_params=_params("parallel", "parallel", "arbitrary"),
        name="dilated_attn",
    )(q, q, q, k, k, k, v, v, v, k, k, k, v, v, v)


def _ret_kernel(q_ref, k_ref, v_ref, g_ref, dm_ref, qd_ref, kd_ref, cd_ref, o_ref,
                state_ref, kv_ref, sbf_ref, *, n_chunks):
    @pl.when(pl.program_id(2) == 0)
    def _():
        state_ref[...] = jnp.zeros_like(state_ref)

    c_len = RET_CHUNK
    rows = [pl.ds(c * c_len, c_len) for c in range(n_chunks)]

    dmask, qdec, kdec, cdec = dm_ref[...], qd_ref[...], kd_ref[...], cd_ref[...]
    kd_t = [(k_ref[r, :].astype(F32) * kdec).T.astype(BF16) for r in rows]
    for c, r in enumerate(rows):
        kv_ref[c] = _mm(kd_t[c], v_ref[r, :])
    scores = [_mm_nt(q_ref[r, :], k_ref[r, :]) for r in rows]

    for i in range(RET_QK_DIM // RET_SLAB):
        rs = pl.ds(i * RET_SLAB, RET_SLAB)
        st = state_ref[rs, :]
        for c in range(n_chunks):
            sbf_ref[c, rs, :] = st.astype(BF16)
            st = st * cdec + kv_ref[c, rs, :]
        state_ref[rs, :] = st

    for c, r in enumerate(rows):
        lhs = jnp.concatenate([(scores[c] * dmask).astype(BF16),
                               (q_ref[r, :].astype(F32) * qdec).astype(BF16)], axis=1)
        rhs = jnp.concatenate([v_ref[r, :], sbf_ref[c]], axis=0)
        o = _mm(lhs, rhs)
        mu = jnp.mean(o, axis=-1, keepdims=True)
        oc = o - mu
        var = jnp.mean(oc * oc, axis=-1, keepdims=True)
        on = oc * lax.rsqrt(var + GN_EPS)
        o_ref[r, :] = (on * g_ref[r, :].astype(F32)).astype(o_ref.dtype)


def _ret_tables():
    c_len = RET_CHUNK
    log_gamma = jnp.log1p(-jnp.exp2(-5.0 - jnp.arange(N_RET_HEADS, dtype=F32)))
    idx = jnp.arange(c_len, dtype=F32)
    rel = idx[:, None] - idx[None, :]
    dmask = jnp.where(rel >= 0, jnp.exp(log_gamma[:, None, None] * jnp.maximum(rel, 0.0)), 0.0)
    qdec = jnp.exp(log_gamma[:, None] * (idx + 1.0)[None, :])
    kdec = jnp.exp(log_gamma[:, None] * (c_len - 1.0 - idx)[None, :])
    cdec = jnp.exp(log_gamma * c_len)
    qdec = jnp.broadcast_to(qdec[:, :, None], (N_RET_HEADS, c_len, RET_QK_DIM))
    kdec = jnp.broadcast_to(kdec[:, :, None], (N_RET_HEADS, c_len, RET_QK_DIM))
    cdec = jnp.broadcast_to(cdec[:, None, None], (N_RET_HEADS, 1, RET_V_DIM))
    return dmask, qdec, kdec, cdec


def retention(q, k, v, g, tables, out_dtype, tc=1024):
    b, s, _ = q.shape
    n_chunks = tc // RET_CHUNK
    qk_spec = pl.BlockSpec((None, tc, RET_QK_DIM), lambda bi, h, j: (bi, j, h))
    v_spec = pl.BlockSpec((None, tc, RET_V_DIM), lambda bi, h, j: (bi, j, h))

    def tab(shape):
        return pl.BlockSpec((None,) + shape, lambda bi, h, j: (h, 0, 0))

    return pl.pallas_call(
        functools.partial(_ret_kernel, n_chunks=n_chunks),
        grid=(b, N_RET_HEADS, s // tc),
        in_specs=[qk_spec, qk_spec, v_spec, v_spec,
                  tab((RET_CHUNK, RET_CHUNK)), tab((RET_CHUNK, RET_QK_DIM)), tab((RET_CHUNK, RET_QK_DIM)),
                  tab((1, RET_V_DIM))],
        out_specs=v_spec,
        out_shape=jax.ShapeDtypeStruct((b, s, RET_V_WIDTH), out_dtype),
        scratch_shapes=[pltpu.VMEM((RET_QK_DIM, RET_V_DIM), F32),
                        pltpu.VMEM((n_chunks, RET_QK_DIM, RET_V_DIM), F32),
                        pltpu.VMEM((n_chunks, RET_QK_DIM, RET_V_DIM), BF16)],
        compiler_params=_params("parallel", "parallel", "arbitrary"),
        name="retention",
    )(q, k, v, g, *tables)


def _merge_kernel(oa_ref, or_ref, pa_ref, pr_ref, ga_ref, gb_ref, o_ref):
    ya = _mm(oa_ref[...], pa_ref[...])
    yr = _mm(or_ref[...], pr_ref[...])
    merged = ga_ref[...].astype(F32) * ya + gb_ref[...].astype(F32) * yr
    o_ref[...] = merged.astype(o_ref.dtype)


def merge(o_a, o_r, p_a, p_r, layer, gates, out_dtype, tm=1024, tn=512):
    t, wa = o_a.shape
    wr = o_r.shape[1]
    d = p_a.shape[2]
    nb = d // tn
    return pl.pallas_call(
        _merge_kernel,
        grid=(t // tm, nb),
        in_specs=[pl.BlockSpec((tm, wa), lambda i, j: (i, 0)),
                  pl.BlockSpec((tm, wr), lambda i, j: (i, 0)),
                  pl.BlockSpec((None, wa, tn), lambda i, j: (layer, 0, j)),
                  pl.BlockSpec((None, wr, tn), lambda i, j: (layer, 0, j)),
                  pl.BlockSpec((tm, tn), lambda i, j: (i, j)),
                  pl.BlockSpec((tm, tn), lambda i, j: (i, nb + j))],
        out_specs=pl.BlockSpec((tm, tn), lambda i, j: (i, j)),
        out_shape=jax.ShapeDtypeStruct((t, d), out_dtype),
        compiler_params=_params("parallel", "arbitrary"),
        name="gated_merge",
    )(o_a, o_r, p_a, p_r, gates, gates)


def _rope_tables(seq, dim):
    inv_freq = 1.0 / (ROPE_THETA ** (jnp.arange(0, dim, 2, dtype=F32) / dim))
    ang = jnp.arange(seq, dtype=F32)[:, None] * inv_freq[None, :]
    return jnp.cos(ang), jnp.sin(ang)


def kernel(x, ffn1_norm, ffn1_w_gate, ffn1_w_up, ffn1_w_down, mix_norm, w_in, w_proj_attn, w_proj_ret,
           w_out, ffn2_norm, ffn2_w_gate, ffn2_w_up, ffn2_w_down, final_norm):
    b, s, d = x.shape
    t = b * s
    depth = w_in.shape[0]
    cos_a, sin_a = _rope_tables(s, HEAD_DIM)
    tab_a = (jnp.concatenate([cos_a, cos_a], axis=1), jnp.concatenate([-sin_a, sin_a], axis=1))
    tab_r = _rope_tables(s, RET_QK_DIM)
    ret_tabs = _ret_tables()
    off = IN_OFFSETS
    bf = lambda w: w.astype(BF16)
    f1 = (bf(ffn1_w_gate), bf(ffn1_w_up), bf(ffn1_w_down))
    f2 = (bf(ffn2_w_gate), bf(ffn2_w_up), bf(ffn2_w_down))
    p_a, p_r, w_o = bf(w_proj_attn), bf(w_proj_ret), bf(w_out)

    x = x.reshape(t, d)
    for layer in range(depth):
        x, h = ffn(x, ffn1_norm[layer], *f1, layer, post_gain=mix_norm[layer], post_dtype=BF16)

        def pj(seg, width, mode, dtype, **kw):
            return proj(h, w_in, layer, off[seg], width, mode, dtype, seq=s, **kw)

        q_a = pj(0, ATTN_WIDTH, "rot128", F32, tables=tab_a, scale=HEAD_DIM ** -0.5)
        k_a = pj(1, ATTN_WIDTH, "rot128", F32, tables=tab_a)
        v_a = pj(2, ATTN_WIDTH, "none", F32)
        q_r = pj(3, RET_QK_WIDTH, "rot256", BF16, tables=tab_r)
        k_r = pj(4, RET_QK_WIDTH, "rot256", BF16, tables=tab_r, scale=RET_QK_DIM ** -0.5)
        v_r = pj(5, RET_V_WIDTH, "none", BF16)
        g_r = pj(6, RET_V_WIDTH, "silu", BF16)
        gates = pj(7, 2 * D_MODEL, "sigmoid", BF16)
        o_a = attention(q_a.reshape(b, s, -1), k_a.reshape(b, s, -1), v_a.reshape(b, s, -1), BF16)
        o_r = retention(q_r.reshape(b, s, -1), k_r.reshape(b, s, -1), v_r.reshape(b, s, -1),
                        g_r.reshape(b, s, -1), ret_tabs, BF16)
        merged = merge(o_a.reshape(t, -1), o_r.reshape(t, -1), p_a, p_r, layer, gates, BF16)
        x = proj(merged, w_o, layer, 0, D_MODEL, "residual", F32, resid=x)
        last = layer == depth - 1
        x = ffn(x, ffn2_norm[layer], *f2, layer, post_gain=final_norm if last else None, post_dtype=F32,
                emit_x=not last)
    return x.reshape(b, s, d)
```

```python
import functools

import jax
import jax.numpy as jnp
import numpy as np
from jax import lax
from jax.experimental import pallas as pl
from jax.experimental.pallas import tpu as pltpu

F32 = jnp.float32
BF16 = jnp.bfloat16

LANES = 128
VMEM_LIMIT_BYTES = 56 * 1024 * 1024

D_MODEL = 2048
HEAD_DIM = 128
ATTN_GROUPS = ((128, 1), (512, 4), (2048, 16))
ATTN_HEADS_PER_GROUP = 4
N_ATTN_HEADS = len(ATTN_GROUPS) * ATTN_HEADS_PER_GROUP
ATTN_WIDTH = N_ATTN_HEADS * HEAD_DIM
ATTN_OUT_WIDTH = ATTN_HEADS_PER_GROUP * HEAD_DIM
BLOCK = 128
ROPE_THETA = 10000.0
NEG_INF = -1e30
N_RET_HEADS = D_MODEL // 256
RET_QK_DIM = 256
RET_V_DIM = 512
RET_QK_WIDTH = N_RET_HEADS * RET_QK_DIM
RET_V_WIDTH = N_RET_HEADS * RET_V_DIM
CHUNK = 128
GN_EPS = 1e-5
D_FF = 5632
RMS_EPS = 1e-6
IN_SPLITS = (ATTN_WIDTH, ATTN_WIDTH, ATTN_WIDTH, RET_QK_WIDTH, RET_QK_WIDTH,
             RET_V_WIDTH, RET_V_WIDTH, D_MODEL, D_MODEL)
IN_OFFSETS = tuple(int(o) for o in np.cumsum((0,) + IN_SPLITS[:-1]))

ATTN_SUPER = BLOCK * ATTN_GROUPS[-1][1]
ATTN_UNROLL = 8
RET_CHUNK = 256
RET_SLAB = 32


def _params(*sem):
    return pltpu.CompilerParams(dimension_semantics=sem, vmem_limit_bytes=VMEM_LIMIT_BYTES)


def _mm(a, b):
    return jnp.dot(a, b, preferred_element_type=F32)


def _sigmoid(x):
    return 0.5 * jnp.tanh(0.5 * x) + 0.5


def _mm_nt(a, b):
    return lax.dot_general(a, b, (((1,), (1,)), ((), ())), preferred_element_type=F32)


def _rms(x, g):
    return x * lax.rsqrt(jnp.mean(x * x, axis=-1, keepdims=True) + RMS_EPS) * g


def _ffn_kernel(x_ref, g_ref, wgu_ref, wd_ref, *rest, emit_x, emit_norm):
    if emit_norm:
        g2_ref, rest = rest[0], rest[1:]
    outs, (h_ref, acc_ref) = rest[:-2], rest[-2:]
    f = pl.program_id(1)

    @pl.when(f == 0)
    def _():
        h_ref[...] = _rms(x_ref[...], g_ref[...]).astype(BF16)
        acc_ref[...] = jnp.zeros_like(acc_ref)

    gu = _mm(h_ref[...], wgu_ref[...])
    tf = gu.shape[1] // 2
    gate, up = gu[:, :tf], gu[:, tf:]
    act = gate * jax.nn.sigmoid(gate) * up
    acc_ref[...] += _mm(act.astype(BF16), wd_ref[...])

    @pl.when(f == pl.num_programs(1) - 1)
    def _():
        y = x_ref[...] + 0.5 * acc_ref[...]
        if emit_x:
            outs[0][...] = y
        if emit_norm:
            outs[-1][...] = _rms(y, g2_ref[...]).astype(outs[-1].dtype)


def ffn_weights(w_gate, w_up, w_down, tf=512):
    depth, d, dff = w_gate.shape
    wgu = jnp.concatenate([w_gate.reshape(depth, d, dff // tf, tf), w_up.reshape(depth, d, dff // tf, tf)], axis=3)
    return wgu.transpose(0, 2, 1, 3).astype(BF16), w_down.astype(BF16)


def ffn(x, g, w_gu, w_down, layer, post_gain=None, post_dtype=None, emit_x=True, tm=512):
    t, d = x.shape
    n_chunks, tf = w_gu.shape[1], w_gu.shape[3] // 2
    emit_norm = post_gain is not None
    row_spec = pl.BlockSpec((tm, d), lambda i, f: (i, 0))
    vec_spec = pl.BlockSpec((1, d), lambda i, f: (0, 0))
    in_specs = [row_spec, vec_spec,
                pl.BlockSpec((None, None, d, 2 * tf), lambda i, f: (layer, f, 0, 0)),
                pl.BlockSpec((None, tf, d), lambda i, f: (layer, f, 0))]
    args = [x, g.reshape(1, d), w_gu, w_down]
    out_specs, out_shape = [], []
    if emit_x:
        out_specs.append(row_spec)
        out_shape.append(jax.ShapeDtypeStruct((t, d), F32))
    if emit_norm:
        in_specs.append(vec_spec)
        args.append(post_gain.reshape(1, d))
        out_specs.append(row_spec)
        out_shape.append(jax.ShapeDtypeStruct((t, d), post_dtype))
    res = pl.pallas_call(
        functools.partial(_ffn_kernel, emit_x=emit_x, emit_norm=emit_norm),
        grid=(t // tm, n_chunks),
        in_specs=in_specs,
        out_specs=out_specs,
        out_shape=out_shape,
        scratch_shapes=[pltpu.VMEM((tm, d), BF16), pltpu.VMEM((tm, d), F32)],
        compiler_params=_params("parallel", "arbitrary"),
        name="ffn",
    )(*args)
    return res if len(res) > 1 else res[0]


def _proj_kernel(a_ref, *refs, mode, scale, tn, n_wblocks):
    w_refs, refs = refs[:n_wblocks], refs[n_wblocks:]
    if scale == "row":
        scale, refs = refs[0][...], refs[1:]
    if mode in ("rot128", "rot256"):
        c_ref, s_ref, o_ref = refs
    elif mode == "residual":
        x_ref, o_ref = refs
    else:
        (o_ref,) = refs
    a = a_ref[...]
    y = jnp.concatenate([_mm(a, w_ref[...].astype(BF16)) for w_ref in w_refs], axis=1)
    if scale is not None:
        y = y * scale
    if mode == "rot128":
        c, s = c_ref[...], s_ref[...]
        for b in range(tn // LANES):
            yb = y[:, b * LANES:(b + 1) * LANES]
            ob = yb * c + pltpu.roll(yb, LANES // 2, axis=1) * s
            o_ref[:, b * LANES:(b + 1) * LANES] = ob.astype(o_ref.dtype)
    elif mode == "rot256":
        c, s = c_ref[...], s_ref[...]
        for b in range(tn // (2 * LANES)):
            y1 = y[:, (2 * b) * LANES:(2 * b + 1) * LANES]
            y2 = y[:, (2 * b + 1) * LANES:(2 * b + 2) * LANES]
            o_ref[:, (2 * b) * LANES:(2 * b + 1) * LANES] = (y1 * c - y2 * s).astype(o_ref.dtype)
            o_ref[:, (2 * b + 1) * LANES:(2 * b + 2) * LANES] = (y2 * c + y1 * s).astype(o_ref.dtype)
    elif mode == "silu":
        o_ref[...] = (y * _sigmoid(y)).astype(o_ref.dtype)
    elif mode == "sigmoid":
        o_ref[...] = _sigmoid(y).astype(o_ref.dtype)
    elif mode == "residual":
        o_ref[...] = (x_ref[...] + y).astype(o_ref.dtype)
    else:
        o_ref[...] = y.astype(o_ref.dtype)


def proj(a, w, layer, col_off, n_cols, mode, out_dtype, *, tables=None, scale=None, resid=None, seq=None,
         tm=2048, wb=512, n_wblocks=1):
    t, k = a.shape
    tn = wb * n_wblocks
    jb = col_off // wb
    assert col_off % wb == 0 and n_cols % tn == 0
    in_specs = [pl.BlockSpec((tm, k), lambda i, j: (i, 0))]
    in_specs += [pl.BlockSpec((None, k, wb), lambda i, j, u=u: (layer, 0, jb + j * n_wblocks + u))
                 for u in range(n_wblocks)]
    args = [a] + [w] * n_wblocks
    if scale is not None and not isinstance(scale, float):
        in_specs.append(pl.BlockSpec((1, tn), lambda i, j: (0, j)))
        args.append(scale)
        scale = "row"
    if mode in ("rot128", "rot256"):
        nsb = seq // tm
        tab_spec = pl.BlockSpec((tm, LANES), lambda i, j: (i % nsb, 0))
        in_specs += [tab_spec, tab_spec]
        args += list(tables)
    elif mode == "residual":
        in_specs.append(pl.BlockSpec((tm, tn), lambda i, j: (i, j)))
        args.append(resid)
    return pl.pallas_call(
        functools.partial(_proj_kernel, mode=mode, scale=scale, tn=tn, n_wblocks=n_wblocks),
        grid=(t // tm, n_cols // tn),
        in_specs=in_specs,
        out_specs=pl.BlockSpec((tm, tn), lambda i, j: (i, j)),
        out_shape=jax.ShapeDtypeStruct((t, n_cols), out_dtype),
        compiler_params=_params("parallel", "arbitrary"),
        name="proj_" + mode,
    )(*args)


def _attn_kernel(q0, q1, q2, k0, k1, k2, v0, v1, v2, kp0, kp1, kp2, vp0, vp1, vp2,
                 o_ref, a0, a1, a2, m0, m1, m2, l0, l1, l2):
    row = lax.broadcasted_iota(jnp.int32, (BLOCK, 2 * BLOCK), 0)
    col = lax.broadcasted_iota(jnp.int32, (BLOCK, 2 * BLOCK), 1)
    band = jnp.logical_and(col >= row, col <= row + BLOCK)
    band_halo = jnp.logical_and(band, col >= jnp.where(pl.program_id(2) > 0, 0, BLOCK))
    ones = jnp.ones((2 * BLOCK, LANES), BF16)

    def tiles(items, a_ref, m_ref, l_ref):
        s = [jnp.where(mask, _mm_nt(q.astype(BF16), jnp.concatenate([kp, kc], axis=0).astype(BF16)), NEG_INF)
             for _, q, kp, kc, _, _, mask in items]
        m = [jnp.max(jnp.maximum(si[:, :BLOCK], si[:, BLOCK:]), axis=1, keepdims=True) for si in s]
        p = [jnp.where(it[6], jnp.exp(si - mi), 0.0).astype(BF16) for it, si, mi in zip(items, s, m)]
        al = [_mm(pi, jnp.concatenate([jnp.concatenate([vp, vc], axis=0).astype(BF16), ones], axis=1))
              for (_, _, _, _, vp, vc, _), pi in zip(items, p)]
        for (rows, *_), ali, mi in zip(items, al, m):
            a_ref[rows, :] = ali[:, :LANES]
            l_ref[rows, :] = ali[:, LANES:]
            m_ref[rows, :] = jnp.broadcast_to(mi, (BLOCK, LANES))

    def group(q, k, v, kp, vp, a_ref, m_ref, l_ref, d):
        span = BLOCK * d
        n_sb = ATTN_SUPER // span
        u = ATTN_UNROLL

        def rows_at(start):
            return pl.ds(start, BLOCK, stride=d) if d > 1 else pl.ds(start, BLOCK)

        def item(start, halo):
            rows = rows_at(start)
            if halo:
                return (rows, q[rows, :], kp[rows, :], k[rows, :], vp[rows, :], v[rows, :], band_halo)
            prev = rows_at(start - span)
            return (rows, q[rows, :], k[prev, :], k[rows, :], v[prev, :], v[rows, :], band)

        def batch(items):
            tiles(items, a_ref, m_ref, l_ref)

        if d >= u:
            per = d // u

            def halo_batch(i, carry):
                batch([item(i * u + j, True) for j in range(u)])
                return carry

            lax.fori_loop(0, per, halo_batch, 0)

            def later(sb, carry):
                for rb in range(per):
                    batch([item(sb * span + rb * u + j, False) for j in range(u)])
                return carry

            if n_sb > 1:
                lax.fori_loop(1, n_sb, later, 0)
        else:
            spb = u // d
            batch([item((j // d) * span + j % d, j < d) for j in range(u)])

            def later(i, carry):
                base = pl.multiple_of(i * (spb * span), spb * span)
                batch([item(base + (j // d) * span + j % d, False) for j in range(u)])
                return carry

            if n_sb > spb:
                lax.fori_loop(1, n_sb // spb, later, 0)

    group(q0, k0, v0, kp0, vp0, a0, m0, l0, ATTN_GROUPS[0][1])
    group(q1, k1, v1, kp1, vp1, a1, m1, l1, ATTN_GROUPS[1][1])
    group(q2, k2, v2, kp2, vp2, a2, m2, l2, ATTN_GROUPS[2][1])

    rc = 256

    def merge(i, carry):
        rows = pl.ds(pl.multiple_of(i * rc, rc), rc)
        ma, mb, mc = m0[rows, :], m1[rows, :], m2[rows, :]
        mx = jnp.maximum(jnp.maximum(ma, mb), mc)
        ea, eb, ec = jnp.exp(ma - mx), jnp.exp(mb - mx), jnp.exp(mc - mx)
        num = ea * a0[rows, :] + eb * a1[rows, :] + ec * a2[rows, :]
        den = ea * l0[rows, :] + eb * l1[rows, :] + ec * l2[rows, :]
        o_ref[rows, :] = (num / den).astype(o_ref.dtype)
        return carry

    lax.fori_loop(0, ATTN_SUPER // rc, merge, 0)


def attention(qk, v, out_dtype):
    b, s, _ = v.shape
    hg = ATTN_HEADS_PER_GROUP
    nsb = s // ATTN_SUPER

    def cur(g, col0):
        return pl.BlockSpec((None, ATTN_SUPER, HEAD_DIM), lambda bi, h, j: (bi, j, col0 + g * hg + h))

    def prev(g, col0):
        span = BLOCK * ATTN_GROUPS[g][1]
        per = ATTN_SUPER // span
        return pl.BlockSpec((None, span, HEAD_DIM),
                            lambda bi, h, j: (bi, jnp.maximum(j * per - 1, 0), col0 + g * hg + h))

    groups, k0 = range(len(ATTN_GROUPS)), N_ATTN_HEADS
    in_specs = ([cur(g, 0) for g in groups] + [cur(g, k0) for g in groups] + [cur(g, 0) for g in groups]
                + [prev(g, k0) for g in groups] + [prev(g, 0) for g in groups])
    scratch = [pltpu.VMEM((ATTN_SUPER, LANES), F32) for _ in range(9)]
    return pl.pallas_call(
        _attn_kernel,
        grid=(b, hg, nsb),
        in_specs=in_specs,
        out_specs=pl.BlockSpec((None, ATTN_SUPER, HEAD_DIM), lambda bi, h, j: (bi, j, h)),
        out_shape=jax.ShapeDtypeStruct((b, s, ATTN_OUT_WIDTH), out_dtype),
        scratch_shapes=scratch,
        compiler_params=_params("parallel", "parallel", "arbitrary"),
        name="dilated_attn",
    )(*([qk] * 6 + [v] * 3 + [qk] * 3 + [v] * 3))


def _ret_kernel(q_ref, k_ref, v_ref, g_ref, dm_ref, qd_ref, kd_ref, cd_ref, o_ref,
                state_ref, kv_ref, sbf_ref, *, n_chunks):
    @pl.when(pl.program_id(2) == 0)
    def _():
        state_ref[...] = jnp.zeros_like(state_ref)

    c_len = RET_CHUNK
    rows = [pl.ds(c * c_len, c_len) for c in range(n_chunks)]

    dmask, qdec, kdec, cdec = dm_ref[...], qd_ref[...], kd_ref[...], cd_ref[...]
    kd_t = [(k_ref[r, :].astype(F32) * kdec).T.astype(BF16) for r in rows]
    for c, r in enumerate(rows):
        kv_ref[c] = _mm(kd_t[c], v_ref[r, :])
    scores = [_mm_nt(q_ref[r, :], k_ref[r, :]) for r in rows]

    for i in range(RET_QK_DIM // RET_SLAB):
        rs = pl.ds(i * RET_SLAB, RET_SLAB)
        st = state_ref[rs, :]
        for c in range(n_chunks):
            sbf_ref[c, rs, :] = st.astype(BF16)
            st = st * cdec + kv_ref[c, rs, :]
        state_ref[rs, :] = st

    for c, r in enumerate(rows):
        lhs = jnp.concatenate([(scores[c] * dmask).astype(BF16),
                               (q_ref[r, :].astype(F32) * qdec).astype(BF16)], axis=1)
        rhs = jnp.concatenate([v_ref[r, :], sbf_ref[c]], axis=0)
        o = _mm(lhs, rhs)
        mu = jnp.mean(o, axis=-1, keepdims=True)
        oc = o - mu
        var = jnp.mean(oc * oc, axis=-1, keepdims=True)
        on = oc * lax.rsqrt(var + GN_EPS)
        o_ref[r, :] = (on * g_ref[r, :].astype(F32)).astype(o_ref.dtype)


def _ret_tables():
    c_len = RET_CHUNK
    log_gamma = jnp.log1p(-jnp.exp2(-5.0 - jnp.arange(N_RET_HEADS, dtype=F32)))
    idx = jnp.arange(c_len, dtype=F32)
    rel = idx[:, None] - idx[None, :]
    dmask = jnp.where(rel >= 0, jnp.exp(log_gamma[:, None, None] * jnp.maximum(rel, 0.0)), 0.0)
    qdec = jnp.exp(log_gamma[:, None] * (idx + 1.0)[None, :])
    kdec = jnp.exp(log_gamma[:, None] * (c_len - 1.0 - idx)[None, :])
    cdec = jnp.exp(log_gamma * c_len)
    qdec = jnp.broadcast_to(qdec[:, :, None], (N_RET_HEADS, c_len, RET_QK_DIM))
    kdec = jnp.broadcast_to(kdec[:, :, None], (N_RET_HEADS, c_len, RET_QK_DIM))
    cdec = jnp.broadcast_to(cdec[:, None, None], (N_RET_HEADS, 1, RET_V_DIM))
    return dmask, qdec, kdec, cdec


def retention(q, k, v, g, tables, out_dtype, tc=2048):
    b, s, _ = q.shape
    n_chunks = tc // RET_CHUNK
    qk_spec = pl.BlockSpec((None, tc, RET_QK_DIM), lambda bi, h, j: (bi, j, h))
    v_spec = pl.BlockSpec((None, tc, RET_V_DIM), lambda bi, h, j: (bi, j, h))

    def tab(shape):
        return pl.BlockSpec((None,) + shape, lambda bi, h, j: (h, 0, 0))

    return pl.pallas_call(
        functools.partial(_ret_kernel, n_chunks=n_chunks),
        grid=(b, N_RET_HEADS, s // tc),
        in_specs=[qk_spec, qk_spec, v_spec, v_spec,
                  tab((RET_CHUNK, RET_CHUNK)), tab((RET_CHUNK, RET_QK_DIM)), tab((RET_CHUNK, RET_QK_DIM)),
                  tab((1, RET_V_DIM))],
        out_specs=v_spec,
        out_shape=jax.ShapeDtypeStruct((b, s, RET_V_WIDTH), out_dtype),
        scratch_shapes=[pltpu.VMEM((RET_QK_DIM, RET_V_DIM), F32),
                        pltpu.VMEM((n_chunks, RET_QK_DIM, RET_V_DIM), F32),
                        pltpu.VMEM((n_chunks, RET_QK_DIM, RET_V_DIM), BF16)],
        compiler_params=_params("parallel", "parallel", "arbitrary"),
        name="retention",
    )(q, k, v, g, *tables)


def _merge_kernel(oa_ref, or_ref, pa_ref, pr_ref, ga_ref, gb_ref, o_ref):
    ya = _mm(oa_ref[...], pa_ref[...])
    yr = _mm(or_ref[...], pr_ref[...])
    merged = ga_ref[...].astype(F32) * ya + gb_ref[...].astype(F32) * yr
    o_ref[...] = merged.astype(o_ref.dtype)


def merge(o_a, o_r, p_a, p_r, layer, gates, out_dtype, tm=1024, tn=512):
    t, wa = o_a.shape
    wr = o_r.shape[1]
    d = p_a.shape[2]
    nb = d // tn
    return pl.pallas_call(
        _merge_kernel,
        grid=(t // tm, nb),
        in_specs=[pl.BlockSpec((tm, wa), lambda i, j: (i, 0)),
                  pl.BlockSpec((tm, wr), lambda i, j: (i, 0)),
                  pl.BlockSpec((None, wa, tn), lambda i, j: (layer, 0, j)),
                  pl.BlockSpec((None, wr, tn), lambda i, j: (layer, 0, j)),
                  pl.BlockSpec((tm, tn), lambda i, j: (i, j)),
                  pl.BlockSpec((tm, tn), lambda i, j: (i, nb + j))],
        out_specs=pl.BlockSpec((tm, tn), lambda i, j: (i, j)),
        out_shape=jax.ShapeDtypeStruct((t, d), out_dtype),
        compiler_params=_params("parallel", "arbitrary"),
        name="gated_merge",
    )(o_a, o_r, p_a, p_r, gates, gates)


def _rope_tables(seq, dim):
    inv_freq = 1.0 / (ROPE_THETA ** (jnp.arange(0, dim, 2, dtype=F32) / dim))
    ang = jnp.arange(seq, dtype=F32)[:, None] * inv_freq[None, :]
    return jnp.cos(ang), jnp.sin(ang)


def kernel(x, ffn1_norm, ffn1_w_gate, ffn1_w_up, ffn1_w_down, mix_norm, w_in, w_proj_attn, w_proj_ret,
           w_out, ffn2_norm, ffn2_w_gate, ffn2_w_up, ffn2_w_down, final_norm):
    b, s, d = x.shape
    t = b * s
    depth = w_in.shape[0]
    cos_a, sin_a = _rope_tables(s, HEAD_DIM)
    tab_a = (jnp.concatenate([cos_a, cos_a], axis=1), jnp.concatenate([-sin_a, sin_a], axis=1))
    tab_r = _rope_tables(s, RET_QK_DIM)
    ret_tabs = _ret_tables()
    off = IN_OFFSETS
    qk_scale = jnp.concatenate([jnp.full((1, ATTN_WIDTH), HEAD_DIM ** -0.5, F32), jnp.ones((1, ATTN_WIDTH), F32)],
                               axis=1)
    bf = lambda w: w.astype(BF16)
    f1 = ffn_weights(ffn1_w_gate, ffn1_w_up, ffn1_w_down)
    f2 = ffn_weights(ffn2_w_gate, ffn2_w_up, ffn2_w_down)
    p_a, p_r, w_o = bf(w_proj_attn), bf(w_proj_ret), bf(w_out)

    x = x.reshape(t, d)
    for layer in range(depth):
        x, h = ffn(x, ffn1_norm[layer], *f1, layer, post_gain=mix_norm[layer], post_dtype=BF16)

        def pj(seg, width, mode, dtype, **kw):
            return proj(h, w_in, layer, off[seg], width, mode, dtype, seq=s, **kw)

        wide = dict(tm=1024, n_wblocks=2)
        qk_a = pj(0, 2 * ATTN_WIDTH, "rot128", F32, tables=tab_a, scale=qk_scale, **wide)
        v_a = pj(2, ATTN_WIDTH, "none", F32)
        q_r = pj(3, RET_QK_WIDTH, "rot256", BF16, tables=tab_r, **wide)
        k_r = pj(4, RET_QK_WIDTH, "rot256", BF16, tables=tab_r, scale=RET_QK_DIM ** -0.5, **wide)
        v_r = pj(5, RET_V_WIDTH, "none", BF16, **wide)
        g_r = pj(6, RET_V_WIDTH, "silu", BF16, **wide)
        gates = pj(7, 2 * D_MODEL, "sigmoid", BF16, **wide)
        o_a = attention(qk_a.reshape(b, s, -1), v_a.reshape(b, s, -1), BF16)
        o_r = retention(q_r.reshape(b, s, -1), k_r.reshape(b, s, -1), v_r.reshape(b, s, -1),
                        g_r.reshape(b, s, -1), ret_tabs, BF16)
        merged = merge(o_a.reshape(t, -1), o_r.reshape(t, -1), p_a, p_r, layer, gates, BF16)
        x = proj(merged, w_o, layer, 0, D_MODEL, "residual", F32, resid=x, tm=512, wb=D_MODEL)
        last = layer == depth - 1
        x = ffn(x, ffn2_norm[layer], *f2, layer, post_gain=final_norm if last else None, post_dtype=F32,
                emit_x=not last)
    return x.reshape(b, s, d)
```

```python
import functools

import jax
import jax.numpy as jnp
import numpy as np
from jax import lax
from jax.experimental import pallas as pl
from jax.experimental.pallas import tpu as pltpu

F32 = jnp.float32
BF16 = jnp.bfloat16

LANES = 128
VMEM_LIMIT_BYTES = 56 * 1024 * 1024

D_MODEL = 2048
HEAD_DIM = 128
ATTN_GROUPS = ((128, 1), (512, 4), (2048, 16))
ATTN_HEADS_PER_GROUP = 4
N_ATTN_HEADS = len(ATTN_GROUPS) * ATTN_HEADS_PER_GROUP
ATTN_WIDTH = N_ATTN_HEADS * HEAD_DIM
ATTN_OUT_WIDTH = ATTN_HEADS_PER_GROUP * HEAD_DIM
BLOCK = 128
ROPE_THETA = 10000.0
NEG_INF = -1e30
N_RET_HEADS = D_MODEL // 256
RET_QK_DIM = 256
RET_V_DIM = 512
RET_QK_WIDTH = N_RET_HEADS * RET_QK_DIM
RET_V_WIDTH = N_RET_HEADS * RET_V_DIM
CHUNK = 128
GN_EPS = 1e-5
D_FF = 5632
RMS_EPS = 1e-6
IN_SPLITS = (ATTN_WIDTH, ATTN_WIDTH, ATTN_WIDTH, RET_QK_WIDTH, RET_QK_WIDTH,
             RET_V_WIDTH, RET_V_WIDTH, D_MODEL, D_MODEL)
IN_OFFSETS = tuple(int(o) for o in np.cumsum((0,) + IN_SPLITS[:-1]))

ATTN_SUPER = BLOCK * ATTN_GROUPS[-1][1]
ATTN_UNROLL = 8
RET_CHUNK = 256
RET_SLAB = 32


def _params(*sem):
    return pltpu.CompilerParams(dimension_semantics=sem, vmem_limit_bytes=VMEM_LIMIT_BYTES)


def _mm(a, b):
    return jnp.dot(a, b, preferred_element_type=F32)


def _sigmoid(x):
    return 0.5 * jnp.tanh(0.5 * x) + 0.5


def _mm_nt(a, b):
    return lax.dot_general(a, b, (((1,), (1,)), ((), ())), preferred_element_type=F32)


def _rms(x, g):
    return x * lax.rsqrt(jnp.mean(x * x, axis=-1, keepdims=True) + RMS_EPS) * g


def _ffn_kernel(x_ref, g_ref, wg_ref, wu_ref, wd_ref, *rest, emit_x, emit_norm):
    if emit_norm:
        g2_ref, rest = rest[0], rest[1:]
    outs, (h_ref, acc_ref) = rest[:-2], rest[-2:]
    f = pl.program_id(1)

    @pl.when(f == 0)
    def _():
        h_ref[...] = _rms(x_ref[...], g_ref[...]).astype(BF16)
        acc_ref[...] = jnp.zeros_like(acc_ref)

    h = h_ref[...]
    gate = _mm(h, wg_ref[...])
    up = _mm(h, wu_ref[...])
    act = gate * jax.nn.sigmoid(gate) * up
    acc_ref[...] += _mm(act.astype(BF16), wd_ref[...])

    @pl.when(f == pl.num_programs(1) - 1)
    def _():
        y = x_ref[...] + 0.5 * acc_ref[...]
        if emit_x:
            outs[0][...] = y
        if emit_norm:
            outs[-1][...] = _rms(y, g2_ref[...]).astype(outs[-1].dtype)


def ffn(x, g, w_gate, w_up, w_down, layer, post_gain=None, post_dtype=None, emit_x=True, tm=512, tf=512):
    t, d = x.shape
    n_chunks = w_gate.shape[2] // tf
    emit_norm = post_gain is not None
    row_spec = pl.BlockSpec((tm, d), lambda i, f: (i, 0))
    vec_spec = pl.BlockSpec((1, d), lambda i, f: (0, 0))
    in_specs = [row_spec, vec_spec,
                pl.BlockSpec((None, d, tf), lambda i, f: (layer, 0, f)),
                pl.BlockSpec((None, d, tf), lambda i, f: (layer, 0, f)),
                pl.BlockSpec((None, tf, d), lambda i, f: (layer, f, 0))]
    args = [x, g.reshape(1, d), w_gate, w_up, w_down]
    out_specs, out_shape = [], []
    if emit_x:
        out_specs.append(row_spec)
        out_shape.append(jax.ShapeDtypeStruct((t, d), F32))
    if emit_norm:
        in_specs.append(vec_spec)
        args.append(post_gain.reshape(1, d))
        out_specs.append(row_spec)
        out_shape.append(jax.ShapeDtypeStruct((t, d), post_dtype))
    res = pl.pallas_call(
        functools.partial(_ffn_kernel, emit_x=emit_x, emit_norm=emit_norm),
        grid=(t // tm, n_chunks),
        in_specs=in_specs,
        out_specs=out_specs,
        out_shape=out_shape,
        scratch_shapes=[pltpu.VMEM((tm, d), BF16), pltpu.VMEM((tm, d), F32)],
        compiler_params=_params("parallel", "arbitrary"),
        name="ffn",
    )(*args)
    return res if len(res) > 1 else res[0]


def _proj_kernel(a_ref, *refs, mode, scale, tn, n_wblocks):
    w_refs, refs = refs[:n_wblocks], refs[n_wblocks:]
    if scale == "row":
        scale, refs = refs[0][...], refs[1:]
    if mode in ("rot128", "rot256"):
        c_ref, s_ref, o_ref = refs
    elif mode == "residual":
        x_ref, o_ref = refs
    else:
        (o_ref,) = refs
    a = a_ref[...]
    y = jnp.concatenate([_mm(a, w_ref[...].astype(BF16)) for w_ref in w_refs], axis=1)
    if scale is not None:
        y = y * scale
    if mode == "rot128":
        c, s = c_ref[...], s_ref[...]
        for b in range(tn // LANES):
            yb = y[:, b * LANES:(b + 1) * LANES]
            ob = yb * c + pltpu.roll(yb, LANES // 2, axis=1) * s
            o_ref[:, b * LANES:(b + 1) * LANES] = ob.astype(o_ref.dtype)
    elif mode == "rot256":
        c, s = c_ref[...], s_ref[...]
        for b in range(tn // (2 * LANES)):
            y1 = y[:, (2 * b) * LANES:(2 * b + 1) * LANES]
            y2 = y[:, (2 * b + 1) * LANES:(2 * b + 2) * LANES]
            o_ref[:, (2 * b) * LANES:(2 * b + 1) * LANES] = (y1 * c - y2 * s).astype(o_ref.dtype)
            o_ref[:, (2 * b + 1) * LANES:(2 * b + 2) * LANES] = (y2 * c + y1 * s).astype(o_ref.dtype)
    elif mode == "sigmoid":
        o_ref[...] = _sigmoid(y).astype(o_ref.dtype)
    elif mode == "residual":
        o_ref[...] = (x_ref[...] + y).astype(o_ref.dtype)
    else:
        o_ref[...] = y.astype(o_ref.dtype)


def proj(a, w, layer, col_off, n_cols, mode, out_dtype, *, tables=None, scale=None, resid=None, seq=None,
         tm=2048, wb=512, n_wblocks=1):
    t, k = a.shape
    tn = wb * n_wblocks
    jb = col_off // wb
    assert col_off % wb == 0 and n_cols % tn == 0
    in_specs = [pl.BlockSpec((tm, k), lambda i, j: (i, 0))]
    in_specs += [pl.BlockSpec((None, k, wb), lambda i, j, u=u: (layer, 0, jb + j * n_wblocks + u))
                 for u in range(n_wblocks)]
    args = [a] + [w] * n_wblocks
    if scale is not None and not isinstance(scale, float):
        in_specs.append(pl.BlockSpec((1, tn), lambda i, j: (0, j)))
        args.append(scale)
        scale = "row"
    if mode in ("rot128", "rot256"):
        nsb = seq // tm
        tab_spec = pl.BlockSpec((tm, LANES), lambda i, j: (i % nsb, 0))
        in_specs += [tab_spec, tab_spec]
        args += list(tables)
    elif mode == "residual":
        in_specs.append(pl.BlockSpec((tm, tn), lambda i, j: (i, j)))
        args.append(resid)
    return pl.pallas_call(
        functools.partial(_proj_kernel, mode=mode, scale=scale, tn=tn, n_wblocks=n_wblocks),
        grid=(t // tm, n_cols // tn),
        in_specs=in_specs,
        out_specs=pl.BlockSpec((tm, tn), lambda i, j: (i, j)),
        out_shape=jax.ShapeDtypeStruct((t, n_cols), out_dtype),
        compiler_params=_params("parallel", "arbitrary"),
        name="proj_" + mode,
    )(*args)


def _attn_kernel(q0, q1, q2, k0, k1, k2, v0, v1, v2, kp0, kp1, kp2, vp0, vp1, vp2,
                 o_ref, a0, a1, a2, m0, m1, m2, l0, l1, l2):
    row = lax.broadcasted_iota(jnp.int32, (BLOCK, 2 * BLOCK), 0)
    col = lax.broadcasted_iota(jnp.int32, (BLOCK, 2 * BLOCK), 1)
    band = jnp.logical_and(col >= row, col <= row + BLOCK)
    band_halo = jnp.logical_and(band, col >= jnp.where(pl.program_id(2) > 0, 0, BLOCK))
    ones = jnp.ones((2 * BLOCK, LANES), BF16)

    def tiles(items, a_ref, m_ref, l_ref):
        s = [jnp.where(mask, _mm_nt(q.astype(BF16), jnp.concatenate([kp, kc], axis=0).astype(BF16)), NEG_INF)
             for _, q, kp, kc, _, _, mask in items]
        m = [jnp.max(jnp.maximum(si[:, :BLOCK], si[:, BLOCK:]), axis=1, keepdims=True) for si in s]
        p = [jnp.where(it[6], jnp.exp(si - mi), 0.0).astype(BF16) for it, si, mi in zip(items, s, m)]
        al = [_mm(pi, jnp.concatenate([jnp.concatenate([vp, vc], axis=0).astype(BF16), ones], axis=1))
              for (_, _, _, _, vp, vc, _), pi in zip(items, p)]
        for (rows, *_), ali, mi in zip(items, al, m):
            a_ref[rows, :] = ali[:, :LANES]
            l_ref[rows, :] = ali[:, LANES:]
            m_ref[rows, :] = jnp.broadcast_to(mi, (BLOCK, LANES))

    def group(q, k, v, kp, vp, a_ref, m_ref, l_ref, d):
        span = BLOCK * d
        n_sb = ATTN_SUPER // span
        u = ATTN_UNROLL

        def rows_at(start):
            return pl.ds(start, BLOCK, stride=d) if d > 1 else pl.ds(start, BLOCK)

        def item(start, halo):
            rows = rows_at(start)
            if halo:
                return (rows, q[rows, :], kp[rows, :], k[rows, :], vp[rows, :], v[rows, :], band_halo)
            prev = rows_at(start - span)
            return (rows, q[rows, :], k[prev, :], k[rows, :], v[prev, :], v[rows, :], band)

        def batch(items):
            tiles(items, a_ref, m_ref, l_ref)

        if d >= u:
            per = d // u

            def halo_batch(i, carry):
                batch([item(i * u + j, True) for j in range(u)])
                return carry

            lax.fori_loop(0, per, halo_batch, 0)

            def later(sb, carry):
                for rb in range(per):
                    batch([item(sb * span + rb * u + j, False) for j in range(u)])
                return carry

            if n_sb > 1:
                lax.fori_loop(1, n_sb, later, 0)
        else:
            spb = u // d
            batch([item((j // d) * span + j % d, j < d) for j in range(u)])

            def later(i, carry):
                base = pl.multiple_of(i * (spb * span), spb * span)
                batch([item(base + (j // d) * span + j % d, False) for j in range(u)])
                return carry

            if n_sb > spb:
                lax.fori_loop(1, n_sb // spb, later, 0)

    group(q0, k0, v0, kp0, vp0, a0, m0, l0, ATTN_GROUPS[0][1])
    group(q1, k1, v1, kp1, vp1, a1, m1, l1, ATTN_GROUPS[1][1])
    group(q2, k2, v2, kp2, vp2, a2, m2, l2, ATTN_GROUPS[2][1])

    rc = 256

    def merge(i, carry):
        rows = pl.ds(pl.multiple_of(i * rc, rc), rc)
        ma, mb, mc = m0[rows, :], m1[rows, :], m2[rows, :]
        mx = jnp.maximum(jnp.maximum(ma, mb), mc)
        ea, eb, ec = jnp.exp(ma - mx), jnp.exp(mb - mx), jnp.exp(mc - mx)
        num = ea * a0[rows, :] + eb * a1[rows, :] + ec * a2[rows, :]
        den = ea * l0[rows, :] + eb * l1[rows, :] + ec * l2[rows, :]
        o_ref[rows, :] = (num / den).astype(o_ref.dtype)
        return carry

    lax.fori_loop(0, ATTN_SUPER // rc, merge, 0)


def attention(qk, v, out_dtype):
    b, s, _ = v.shape
    hg = ATTN_HEADS_PER_GROUP
    nsb = s // ATTN_SUPER

    def cur(g, col0):
        return pl.BlockSpec((None, ATTN_SUPER, HEAD_DIM), lambda bi, h, j: (bi, j, col0 + g * hg + h))

    def prev(g, col0):
        span = BLOCK * ATTN_GROUPS[g][1]
        per = ATTN_SUPER // span
        return pl.BlockSpec((None, span, HEAD_DIM),
                            lambda bi, h, j: (bi, jnp.maximum(j * per - 1, 0), col0 + g * hg + h))

    groups, k0 = range(len(ATTN_GROUPS)), N_ATTN_HEADS
    in_specs = ([cur(g, 0) for g in groups] + [cur(g, k0) for g in groups] + [cur(g, 0) for g in groups]
                + [prev(g, k0) for g in groups] + [prev(g, 0) for g in groups])
    scratch = [pltpu.VMEM((ATTN_SUPER, LANES), F32) for _ in range(9)]
    return pl.pallas_call(
        _attn_kernel,
        grid=(b, hg, nsb),
        in_specs=in_specs,
        out_specs=pl.BlockSpec((None, ATTN_SUPER, HEAD_DIM), lambda bi, h, j: (bi, j, h)),
        out_shape=jax.ShapeDtypeStruct((b, s, ATTN_OUT_WIDTH), out_dtype),
        scratch_shapes=scratch,
        compiler_params=_params("parallel", "parallel", "arbitrary"),
        name="dilated_attn",
    )(*([qk] * 6 + [v] * 3 + [qk] * 3 + [v] * 3))


def _ret_kernel(q_ref, k_ref, v_ref, hn_ref, wg_ref, dm_ref, qd_ref, kd_ref, cd_ref, o_ref,
                state_ref, kv_ref, sbf_ref, gate_ref, *, n_chunks):
    @pl.when(pl.program_id(2) == 0)
    def _():
        state_ref[...] = jnp.zeros_like(state_ref)

    c_len = RET_CHUNK
    rows = [pl.ds(c * c_len, c_len) for c in range(n_chunks)]

    dmask, qdec, kdec, cdec = dm_ref[...], qd_ref[...], kd_ref[...], cd_ref[...]
    kd_t = [(k_ref[r, :].astype(F32) * kdec).T.astype(BF16) for r in rows]
    for c, r in enumerate(rows):
        kv_ref[c] = _mm(kd_t[c], v_ref[r, :])
    scores = [_mm_nt(q_ref[r, :], k_ref[r, :]) for r in rows]
    gate_ref[...] = _mm(hn_ref[...], wg_ref[...].astype(BF16))

    for i in range(RET_QK_DIM // RET_SLAB):
        rs = pl.ds(i * RET_SLAB, RET_SLAB)
        st = state_ref[rs, :]
        for c in range(n_chunks):
            sbf_ref[c, rs, :] = st.astype(BF16)
            st = st * cdec + kv_ref[c, rs, :]
        state_ref[rs, :] = st

    for c, r in enumerate(rows):
        lhs = jnp.concatenate([(scores[c] * dmask).astype(BF16),
                               (q_ref[r, :].astype(F32) * qdec).astype(BF16)], axis=1)
        rhs = jnp.concatenate([v_ref[r, :], sbf_ref[c]], axis=0)
        o = _mm(lhs, rhs)
        mu = jnp.mean(o, axis=-1, keepdims=True)
        oc = o - mu
        var = jnp.mean(oc * oc, axis=-1, keepdims=True)
        on = oc * lax.rsqrt(var + GN_EPS)
        g = gate_ref[r, :]
        o_ref[r, :] = (on * (g * _sigmoid(g))).astype(o_ref.dtype)


def _ret_tables():
    c_len = RET_CHUNK
    log_gamma = jnp.log1p(-jnp.exp2(-5.0 - jnp.arange(N_RET_HEADS, dtype=F32)))
    idx = jnp.arange(c_len, dtype=F32)
    rel = idx[:, None] - idx[None, :]
    dmask = jnp.where(rel >= 0, jnp.exp(log_gamma[:, None, None] * jnp.maximum(rel, 0.0)), 0.0)
    qdec = jnp.exp(log_gamma[:, None] * (idx + 1.0)[None, :])
    kdec = jnp.exp(log_gamma[:, None] * (c_len - 1.0 - idx)[None, :])
    cdec = jnp.exp(log_gamma * c_len)
    qdec = jnp.broadcast_to(qdec[:, :, None], (N_RET_HEADS, c_len, RET_QK_DIM))
    kdec = jnp.broadcast_to(kdec[:, :, None], (N_RET_HEADS, c_len, RET_QK_DIM))
    cdec = jnp.broadcast_to(cdec[:, None, None], (N_RET_HEADS, 1, RET_V_DIM))
    return dmask, qdec, kdec, cdec


def retention(q, k, v, hn, w_in, layer, gate_off, tables, out_dtype, tc=2048):
    b, s, _ = q.shape
    d = hn.shape[2]
    n_chunks = tc // RET_CHUNK
    gate_blk = gate_off // RET_V_DIM
    assert gate_off % RET_V_DIM == 0
    qk_spec = pl.BlockSpec((None, tc, RET_QK_DIM), lambda bi, h, j: (bi, j, h))
    v_spec = pl.BlockSpec((None, tc, RET_V_DIM), lambda bi, h, j: (bi, j, h))
    hn_spec = pl.BlockSpec((None, tc, d), lambda bi, h, j: (bi, j, 0))
    wg_spec = pl.BlockSpec((None, d, RET_V_DIM), lambda bi, h, j: (layer, 0, gate_blk + h))

    def tab(shape):
        return pl.BlockSpec((None,) + shape, lambda bi, h, j: (h, 0, 0))

    return pl.pallas_call(
        functools.partial(_ret_kernel, n_chunks=n_chunks),
        grid=(b, N_RET_HEADS, s // tc),
        in_specs=[qk_spec, qk_spec, v_spec, hn_spec, wg_spec,
                  tab((RET_CHUNK, RET_CHUNK)), tab((RET_CHUNK, RET_QK_DIM)), tab((RET_CHUNK, RET_QK_DIM)),
                  tab((1, RET_V_DIM))],
        out_specs=v_spec,
        out_shape=jax.ShapeDtypeStruct((b, s, RET_V_WIDTH), out_dtype),
        scratch_shapes=[pltpu.VMEM((RET_QK_DIM, RET_V_DIM), F32),
                        pltpu.VMEM((n_chunks, RET_QK_DIM, RET_V_DIM), F32),
                        pltpu.VMEM((n_chunks, RET_QK_DIM, RET_V_DIM), BF16),
                        pltpu.VMEM((tc, RET_V_DIM), F32)],
        compiler_params=_params("parallel", "parallel", "arbitrary"),
        name="retention",
    )(q, k, v, hn, w_in, *tables)


def _merge_kernel(oa_ref, or_ref, pa_ref, pr_ref, ga_ref, gb_ref, o_ref):
    ya = _mm(oa_ref[...], pa_ref[...])
    yr = _mm(or_ref[...], pr_ref[...])
    merged = ga_ref[...].astype(F32) * ya + gb_ref[...].astype(F32) * yr
    o_ref[...] = merged.astype(o_ref.dtype)


def merge(o_a, o_r, p_a, p_r, layer, gates, out_dtype, tm=1024, tn=512):
    t, wa = o_a.shape
    wr = o_r.shape[1]
    d = p_a.shape[2]
    nb = d // tn
    return pl.pallas_call(
        _merge_kernel,
        grid=(t // tm, nb),
        in_specs=[pl.BlockSpec((tm, wa), lambda i, j: (i, 0)),
                  pl.BlockSpec((tm, wr), lambda i, j: (i, 0)),
                  pl.BlockSpec((None, wa, tn), lambda i, j: (layer, 0, j)),
                  pl.BlockSpec((None, wr, tn), lambda i, j: (layer, 0, j)),
                  pl.BlockSpec((tm, tn), lambda i, j: (i, j)),
                  pl.BlockSpec((tm, tn), lambda i, j: (i, nb + j))],
        out_specs=pl.BlockSpec((tm, tn), lambda i, j: (i, j)),
        out_shape=jax.ShapeDtypeStruct((t, d), out_dtype),
        compiler_params=_params("parallel", "arbitrary"),
        name="gated_merge",
    )(o_a, o_r, p_a, p_r, gates, gates)


def _rope_tables(seq, dim):
    inv_freq = 1.0 / (ROPE_THETA ** (jnp.arange(0, dim, 2, dtype=F32) / dim))
    ang = jnp.arange(seq, dtype=F32)[:, None] * inv_freq[None, :]
    return jnp.cos(ang), jnp.sin(ang)


def kernel(x, ffn1_norm, ffn1_w_gate, ffn1_w_up, ffn1_w_down, mix_norm, w_in, w_proj_attn, w_proj_ret,
           w_out, ffn2_norm, ffn2_w_gate, ffn2_w_up, ffn2_w_down, final_norm):
    b, s, d = x.shape
    t = b * s
    depth = w_in.shape[0]
    cos_a, sin_a = _rope_tables(s, HEAD_DIM)
    tab_a = (jnp.concatenate([cos_a, cos_a], axis=1), jnp.concatenate([-sin_a, sin_a], axis=1))
    tab_r = _rope_tables(s, RET_QK_DIM)
    ret_tabs = _ret_tables()
    off = IN_OFFSETS
    qk_scale = jnp.concatenate([jnp.full((1, ATTN_WIDTH), HEAD_DIM ** -0.5, F32), jnp.ones((1, ATTN_WIDTH), F32)],
                               axis=1)
    bf = lambda w: w.astype(BF16)
    f1 = (bf(ffn1_w_gate), bf(ffn1_w_up), bf(ffn1_w_down))
    f2 = (bf(ffn2_w_gate), bf(ffn2_w_up), bf(ffn2_w_down))
    p_a, p_r, w_o = bf(w_proj_attn), bf(w_proj_ret), bf(w_out)

    x = x.reshape(t, d)
    for layer in range(depth):
        x, h = ffn(x, ffn1_norm[layer], *f1, layer, post_gain=mix_norm[layer], post_dtype=BF16)

        def pj(seg, width, mode, dtype, **kw):
            return proj(h, w_in, layer, off[seg], width, mode, dtype, seq=s, **kw)

        wide = dict(tm=1024, n_wblocks=2)
        qk_a = pj(0, 2 * ATTN_WIDTH, "rot128", F32, tables=tab_a, scale=qk_scale, **wide)
        v_a = pj(2, ATTN_WIDTH, "none", F32)
        q_r = pj(3, RET_QK_WIDTH, "rot256", BF16, tables=tab_r, **wide)
        k_r = pj(4, RET_QK_WIDTH, "rot256", BF16, tables=tab_r, scale=RET_QK_DIM ** -0.5, **wide)
        v_r = pj(5, RET_V_WIDTH, "none", BF16, **wide)
        gates = pj(7, 2 * D_MODEL, "sigmoid", BF16, **wide)
        o_a = attention(qk_a.reshape(b, s, -1), v_a.reshape(b, s, -1), BF16)
        o_r = retention(q_r.reshape(b, s, -1), k_r.reshape(b, s, -1), v_r.reshape(b, s, -1),
                        h.reshape(b, s, -1), w_in, layer, off[6], ret_tabs, BF16)
        merged = merge(o_a.reshape(t, -1), o_r.reshape(t, -1), p_a, p_r, layer, gates, BF16)
        x = proj(merged, w_o, layer, 0, D_MODEL, "residual", F32, resid=x, tm=512, wb=D_MODEL)
        last = layer == depth - 1
        x = ffn(x, ffn2_norm[layer], *f2, layer, post_gain=final_norm if last else None, post_dtype=F32,
                emit_x=not last)
    return x.reshape(b, s, d)
```

```python
import functools

import jax
import jax.numpy as jnp
import numpy as np
from jax import lax
from jax.experimental import pallas as pl
from jax.experimental.pallas import tpu as pltpu

F32 = jnp.float32
BF16 = jnp.bfloat16

LANES = 128
VMEM_LIMIT_BYTES = 56 * 1024 * 1024
FFN_VMEM_LIMIT_BYTES = 60 * 1024 * 1024

D_MODEL = 2048
HEAD_DIM = 128
ATTN_GROUPS = ((128, 1), (512, 4), (2048, 16))
ATTN_HEADS_PER_GROUP = 4
N_ATTN_HEADS = len(ATTN_GROUPS) * ATTN_HEADS_PER_GROUP
ATTN_WIDTH = N_ATTN_HEADS * HEAD_DIM
ATTN_OUT_WIDTH = ATTN_HEADS_PER_GROUP * HEAD_DIM
BLOCK = 128
ROPE_THETA = 10000.0
NEG_INF = -1e30
N_RET_HEADS = D_MODEL // 256
RET_QK_DIM = 256
RET_V_DIM = 512
RET_QK_WIDTH = N_RET_HEADS * RET_QK_DIM
RET_V_WIDTH = N_RET_HEADS * RET_V_DIM
CHUNK = 128
GN_EPS = 1e-5
D_FF = 5632
RMS_EPS = 1e-6
IN_SPLITS = (ATTN_WIDTH, ATTN_WIDTH, ATTN_WIDTH, RET_QK_WIDTH, RET_QK_WIDTH,
             RET_V_WIDTH, RET_V_WIDTH, D_MODEL, D_MODEL)
IN_OFFSETS = tuple(int(o) for o in np.cumsum((0,) + IN_SPLITS[:-1]))

ATTN_SUPER = BLOCK * ATTN_GROUPS[-1][1]
ATTN_UNROLL = 8
RET_CHUNK = 256
RET_SLAB = 32


def _params(*sem, vmem_limit_bytes=VMEM_LIMIT_BYTES):
    return pltpu.CompilerParams(dimension_semantics=sem, vmem_limit_bytes=vmem_limit_bytes)


def _mm(a, b):
    return jnp.dot(a, b, preferred_element_type=F32)


def _sigmoid(x):
    return 0.5 * jnp.tanh(0.5 * x) + 0.5


def _mm_nt(a, b):
    return lax.dot_general(a, b, (((1,), (1,)), ((), ())), preferred_element_type=F32)


def _rms(x, g):
    return x * lax.rsqrt(jnp.mean(x * x, axis=-1, keepdims=True) + RMS_EPS) * g


def _ffn_kernel(x_ref, g_ref, wg_ref, wu_ref, wd_ref, *rest, emit_x, emit_norm):
    if emit_norm:
        g2_ref, rest = rest[0], rest[1:]
    outs, h_ref = rest[:-1], rest[-1]
    y_ref = outs[0]
    f = pl.program_id(1)

    @pl.when(f == 0)
    def _():
        x = x_ref[...]
        h_ref[...] = _rms(x, g_ref[...]).astype(BF16)
        y_ref[...] = x

    h = h_ref[...]
    gate = _mm(h, wg_ref[...])
    up = _mm(h, wu_ref[...])
    act = gate * jax.nn.sigmoid(gate) * up
    y_ref[...] += 0.5 * _mm(act.astype(BF16), wd_ref[...])

    if emit_norm:
        @pl.when(f == pl.num_programs(1) - 1)
        def _():
            outs[-1][...] = _rms(y_ref[...], g2_ref[...]).astype(outs[-1].dtype)


def ffn(x, g, w_gate, w_up, w_down, layer, post_gain=None, post_dtype=None, emit_x=True, tm=1024, tf=512):
    t, d = x.shape
    n_chunks = w_gate.shape[2] // tf
    emit_norm = post_gain is not None
    assert emit_x or post_dtype == F32
    row_spec = pl.BlockSpec((tm, d), lambda i, f: (i, 0))
    vec_spec = pl.BlockSpec((1, d), lambda i, f: (0, 0))
    in_specs = [row_spec, vec_spec,
                pl.BlockSpec((None, d, tf), lambda i, f: (layer, 0, f)),
                pl.BlockSpec((None, d, tf), lambda i, f: (layer, 0, f)),
                pl.BlockSpec((None, tf, d), lambda i, f: (layer, f, 0))]
    args = [x, g.reshape(1, d), w_gate, w_up, w_down]
    out_specs, out_shape = [], []
    if emit_x:
        out_specs.append(row_spec)
        out_shape.append(jax.ShapeDtypeStruct((t, d), F32))
    if emit_norm:
        in_specs.append(vec_spec)
        args.append(post_gain.reshape(1, d))
        out_specs.append(pl.BlockSpec((tm, d), lambda i, f: (i, 0), pipeline_mode=pl.Buffered(1)) if emit_x
                         else row_spec)
        out_shape.append(jax.ShapeDtypeStruct((t, d), post_dtype))
    res = pl.pallas_call(
        functools.partial(_ffn_kernel, emit_x=emit_x, emit_norm=emit_norm),
        grid=(t // tm, n_chunks),
        in_specs=in_specs,
        out_specs=out_specs,
        out_shape=out_shape,
        scratch_shapes=[pltpu.VMEM((tm, d), BF16)],
        compiler_params=_params("parallel", "arbitrary", vmem_limit_bytes=FFN_VMEM_LIMIT_BYTES),
        name="ffn",
    )(*args)
    return res if len(res) > 1 else res[0]


def _proj_kernel(a_ref, *refs, mode, scale, tn, n_wblocks):
    w_refs, refs = refs[:n_wblocks], refs[n_wblocks:]
    if scale == "row":
        scale, refs = refs[0][...], refs[1:]
    if mode in ("rot128", "rot256"):
        c_ref, s_ref, o_ref = refs
    elif mode == "residual":
        x_ref, o_ref = refs
    else:
        (o_ref,) = refs
    a = a_ref[...]
    y = jnp.concatenate([_mm(a, w_ref[...].astype(BF16)) for w_ref in w_refs], axis=1)
    if scale is not None:
        y = y * scale
    if mode == "rot128":
        c, s = c_ref[...], s_ref[...]
        for b in range(tn // LANES):
            yb = y[:, b * LANES:(b + 1) * LANES]
            ob = yb * c + pltpu.roll(yb, LANES // 2, axis=1) * s
            o_ref[:, b * LANES:(b + 1) * LANES] = ob.astype(o_ref.dtype)
    elif mode == "rot256":
        c, s = c_ref[...], s_ref[...]
        for b in range(tn // (2 * LANES)):
            y1 = y[:, (2 * b) * LANES:(2 * b + 1) * LANES]
            y2 = y[:, (2 * b + 1) * LANES:(2 * b + 2) * LANES]
            o_ref[:, (2 * b) * LANES:(2 * b + 1) * LANES] = (y1 * c - y2 * s).astype(o_ref.dtype)
            o_ref[:, (2 * b + 1) * LANES:(2 * b + 2) * LANES] = (y2 * c + y1 * s).astype(o_ref.dtype)
    elif mode == "sigmoid":
        o_ref[...] = _sigmoid(y).astype(o_ref.dtype)
    elif mode == "residual":
        o_ref[...] = (x_ref[...] + y).astype(o_ref.dtype)
    else:
        o_ref[...] = y.astype(o_ref.dtype)


def proj(a, w, layer, col_off, n_cols, mode, out_dtype, *, tables=None, scale=None, resid=None, seq=None,
         tm=2048, wb=512, n_wblocks=1):
    t, k = a.shape
    tn = wb * n_wblocks
    jb = col_off // wb
    assert col_off % wb == 0 and n_cols % tn == 0
    in_specs = [pl.BlockSpec((tm, k), lambda i, j: (i, 0))]
    in_specs += [pl.BlockSpec((None, k, wb), lambda i, j, u=u: (layer, 0, jb + j * n_wblocks + u))
                 for u in range(n_wblocks)]
    args = [a] + [w] * n_wblocks
    if scale is not None and not isinstance(scale, float):
        in_specs.append(pl.BlockSpec((1, tn), lambda i, j: (0, j)))
        args.append(scale)
        scale = "row"
    if mode in ("rot128", "rot256"):
        nsb = seq // tm
        tab_spec = pl.BlockSpec((tm, LANES), lambda i, j: (i % nsb, 0))
        in_specs += [tab_spec, tab_spec]
        args += list(tables)
    elif mode == "residual":
        in_specs.append(pl.BlockSpec((tm, tn), lambda i, j: (i, j)))
        args.append(resid)
    return pl.pallas_call(
        functools.partial(_proj_kernel, mode=mode, scale=scale, tn=tn, n_wblocks=n_wblocks),
        grid=(t // tm, n_cols // tn),
        in_specs=in_specs,
        out_specs=pl.BlockSpec((tm, tn), lambda i, j: (i, j)),
        out_shape=jax.ShapeDtypeStruct((t, n_cols), out_dtype),
        compiler_params=_params("parallel", "arbitrary"),
        name="proj_" + mode,
    )(*args)


def _attn_kernel(q0, q1, q2, k0, k1, k2, v0, v1, v2, kp0, kp1, kp2, vp0, vp1, vp2,
                 o_ref, a0, a1, a2, m0, m1, m2, l0, l1, l2):
    row = lax.broadcasted_iota(jnp.int32, (BLOCK, 2 * BLOCK), 0)
    col = lax.broadcasted_iota(jnp.int32, (BLOCK, 2 * BLOCK), 1)
    band = jnp.logical_and(col >= row, col <= row + BLOCK)
    band_halo = jnp.logical_and(band, col >= jnp.where(pl.program_id(2) > 0, 0, BLOCK))
    ones = jnp.ones((2 * BLOCK, LANES), BF16)

    def tiles(items, a_ref, m_ref, l_ref):
        s = [jnp.where(mask, _mm_nt(q.astype(BF16), jnp.concatenate([kp, kc], axis=0).astype(BF16)), NEG_INF)
             for _, q, kp, kc, _, _, mask in items]
        m = [jnp.max(jnp.maximum(si[:, :BLOCK], si[:, BLOCK:]), axis=1, keepdims=True) for si in s]
        p = [jnp.where(it[6], jnp.exp(si - mi), 0.0).astype(BF16) for it, si, mi in zip(items, s, m)]
        al = [_mm(pi, jnp.concatenate([jnp.concatenate([vp, vc], axis=0).astype(BF16), ones], axis=1))
              for (_, _, _, _, vp, vc, _), pi in zip(items, p)]
        for (rows, *_), ali, mi in zip(items, al, m):
            a_ref[rows, :] = ali[:, :LANES]
            l_ref[rows, :] = ali[:, LANES:]
            m_ref[rows, :] = jnp.broadcast_to(mi, (BLOCK, LANES))

    def group(q, k, v, kp, vp, a_ref, m_ref, l_ref, d):
        span = BLOCK * d
        n_sb = ATTN_SUPER // span
        u = ATTN_UNROLL

        def rows_at(start):
            return pl.ds(start, BLOCK, stride=d) if d > 1 else pl.ds(start, BLOCK)

        def item(start, halo):
            rows = rows_at(start)
            if halo:
                return (rows, q[rows, :], kp[rows, :], k[rows, :], vp[rows, :], v[rows, :], band_halo)
            prev = rows_at(start - span)
            return (rows, q[rows, :], k[prev, :], k[rows, :], v[prev, :], v[rows, :], band)

        def batch(items):
            tiles(items, a_ref, m_ref, l_ref)

        if d >= u:
            per = d // u

            def halo_batch(i, carry):
                batch([item(i * u + j, True) for j in range(u)])
                return carry

            lax.fori_loop(0, per, halo_batch, 0)

            def later(sb, carry):
                for rb in range(per):
                    batch([item(sb * span + rb * u + j, False) for j in range(u)])
                return carry

            if n_sb > 1:
                lax.fori_loop(1, n_sb, later, 0)
        else:
            spb = u // d
            batch([item((j // d) * span + j % d, j < d) for j in range(u)])

            def later(i, carry):
                base = pl.multiple_of(i * (spb * span), spb * span)
                batch([item(base + (j // d) * span + j % d, False) for j in range(u)])
                return carry

            if n_sb > spb:
                lax.fori_loop(1, n_sb // spb, later, 0)

    group(q0, k0, v0, kp0, vp0, a0, m0, l0, ATTN_GROUPS[0][1])
    group(q1, k1, v1, kp1, vp1, a1, m1, l1, ATTN_GROUPS[1][1])
    group(q2, k2, v2, kp2, vp2, a2, m2, l2, ATTN_GROUPS[2][1])

    rc = 256

    def merge(i, carry):
        rows = pl.ds(pl.multiple_of(i * rc, rc), rc)
        ma, mb, mc = m0[rows, :], m1[rows, :], m2[rows, :]
        mx = jnp.maximum(jnp.maximum(ma, mb), mc)
        ea, eb, ec = jnp.exp(ma - mx), jnp.exp(mb - mx), jnp.exp(mc - mx)
        num = ea * a0[rows, :] + eb * a1[rows, :] + ec * a2[rows, :]
        den = ea * l0[rows, :] + eb * l1[rows, :] + ec * l2[rows, :]
        o_ref[rows, :] = (num / den).astype(o_ref.dtype)
        return carry

    lax.fori_loop(0, ATTN_SUPER // rc, merge, 0)


def attention(qk, v, out_dtype):
    b, s, _ = v.shape
    hg = ATTN_HEADS_PER_GROUP
    nsb = s // ATTN_SUPER

    def cur(g, col0):
        return pl.BlockSpec((None, ATTN_SUPER, HEAD_DIM), lambda bi, h, j: (bi, j, col0 + g * hg + h))

    def prev(g, col0):
        span = BLOCK * ATTN_GROUPS[g][1]
        per = ATTN_SUPER // span
        return pl.BlockSpec((None, span, HEAD_DIM),
                            lambda bi, h, j: (bi, jnp.maximum(j * per - 1, 0), col0 + g * hg + h))

    groups, k0 = range(len(ATTN_GROUPS)), N_ATTN_HEADS
    in_specs = ([cur(g, 0) for g in groups] + [cur(g, k0) for g in groups] + [cur(g, 0) for g in groups]
                + [prev(g, k0) for g in groups] + [prev(g, 0) for g in groups])
    scratch = [pltpu.VMEM((ATTN_SUPER, LANES), F32) for _ in range(9)]
    return pl.pallas_call(
        _attn_kernel,
        grid=(b, hg, nsb),
        in_specs=in_specs,
        out_specs=pl.BlockSpec((None, ATTN_SUPER, HEAD_DIM), lambda bi, h, j: (bi, j, h)),
        out_shape=jax.ShapeDtypeStruct((b, s, ATTN_OUT_WIDTH), out_dtype),
        scratch_shapes=scratch,
        compiler_params=_params("parallel", "parallel", "arbitrary"),
        name="dilated_attn",
    )(*([qk] * 6 + [v] * 3 + [qk] * 3 + [v] * 3))


def _ret_kernel(q_ref, k_ref, v_ref, hn_ref, wg_ref, dm_ref, qd_ref, kd_ref, cd_ref, o_ref,
                state_ref, kv_ref, sbf_ref, gate_ref, *, n_chunks):
    @pl.when(pl.program_id(2) == 0)
    def _():
        state_ref[...] = jnp.zeros_like(state_ref)

    c_len = RET_CHUNK
    rows = [pl.ds(c * c_len, c_len) for c in range(n_chunks)]

    dmask, qdec, kdec, cdec = dm_ref[...], qd_ref[...], kd_ref[...], cd_ref[...]
    kd_t = [(k_ref[r, :].astype(F32) * kdec).T.astype(BF16) for r in rows]
    for c, r in enumerate(rows):
        kv_ref[c] = _mm(kd_t[c], v_ref[r, :])
    scores = [_mm_nt(q_ref[r, :], k_ref[r, :]) for r in rows]
    gate_ref[...] = _mm(hn_ref[...], wg_ref[...].astype(BF16))

    for i in range(RET_QK_DIM // RET_SLAB):
        rs = pl.ds(i * RET_SLAB, RET_SLAB)
        st = state_ref[rs, :]
        for c in range(n_chunks):
            sbf_ref[c, rs, :] = st.astype(BF16)
            st = st * cdec + kv_ref[c, rs, :]
        state_ref[rs, :] = st

    for c, r in enumerate(rows):
        lhs = jnp.concatenate([(scores[c] * dmask).astype(BF16),
                               (q_ref[r, :].astype(F32) * qdec).astype(BF16)], axis=1)
        rhs = jnp.concatenate([v_ref[r, :], sbf_ref[c]], axis=0)
        o = _mm(lhs, rhs)
        mu = jnp.mean(o, axis=-1, keepdims=True)
        oc = o - mu
        var = jnp.mean(oc * oc, axis=-1, keepdims=True)
        on = oc * lax.rsqrt(var + GN_EPS)
        g = gate_ref[r, :]
        o_ref[r, :] = (on * (g * _sigmoid(g))).astype(o_ref.dtype)


def _ret_tables():
    c_len = RET_CHUNK
    log_gamma = jnp.log1p(-jnp.exp2(-5.0 - jnp.arange(N_RET_HEADS, dtype=F32)))
    idx = jnp.arange(c_len, dtype=F32)
    rel = idx[:, None] - idx[None, :]
    dmask = jnp.where(rel >= 0, jnp.exp(log_gamma[:, None, None] * jnp.maximum(rel, 0.0)), 0.0)
    qdec = jnp.exp(log_gamma[:, None] * (idx + 1.0)[None, :])
    kdec = jnp.exp(log_gamma[:, None] * (c_len - 1.0 - idx)[None, :])
    cdec = jnp.exp(log_gamma * c_len)
    qdec = jnp.broadcast_to(qdec[:, :, None], (N_RET_HEADS, c_len, RET_QK_DIM))
    kdec = jnp.broadcast_to(kdec[:, :, None], (N_RET_HEADS, c_len, RET_QK_DIM))
    cdec = jnp.broadcast_to(cdec[:, None, None], (N_RET_HEADS, 1, RET_V_DIM))
    return dmask, qdec, kdec, cdec


def retention(q, k, v, hn, w_in, layer, gate_off, tables, out_dtype, tc=2048):
    b, s, _ = q.shape
    d = hn.shape[2]
    n_chunks = tc // RET_CHUNK
    gate_blk = gate_off // RET_V_DIM
    assert gate_off % RET_V_DIM == 0
    qk_spec = pl.BlockSpec((None, tc, RET_QK_DIM), lambda bi, h, j: (bi, j, h))
    v_spec = pl.BlockSpec((None, tc, RET_V_DIM), lambda bi, h, j: (bi, j, h))
    hn_spec = pl.BlockSpec((None, tc, d), lambda bi, h, j: (bi, j, 0))
    wg_spec = pl.BlockSpec((None, d, RET_V_DIM), lambda bi, h, j: (layer, 0, gate_blk + h))

    def tab(shape):
        return pl.BlockSpec((None,) + shape, lambda bi, h, j: (h, 0, 0))

    return pl.pallas_call(
        functools.partial(_ret_kernel, n_chunks=n_chunks),
        grid=(b, N_RET_HEADS, s // tc),
        in_specs=[qk_spec, qk_spec, v_spec, hn_spec, wg_spec,
                  tab((RET_CHUNK, RET_CHUNK)), tab((RET_CHUNK, RET_QK_DIM)), tab((RET_CHUNK, RET_QK_DIM)),
                  tab((1, RET_V_DIM))],
        out_specs=v_spec,
        out_shape=jax.ShapeDtypeStruct((b, s, RET_V_WIDTH), out_dtype),
        scratch_shapes=[pltpu.VMEM((RET_QK_DIM, RET_V_DIM), F32),
                        pltpu.VMEM((n_chunks, RET_QK_DIM, RET_V_DIM), F32),
                        pltpu.VMEM((n_chunks, RET_QK_DIM, RET_V_DIM), BF16),
                        pltpu.VMEM((tc, RET_V_DIM), F32)],
        compiler_params=_params("parallel", "parallel", "arbitrary"),
        name="retention",
    )(q, k, v, hn, w_in, *tables)


def _merge_kernel(oa_ref, or_ref, pa_ref, pr_ref, ga_ref, gb_ref, o_ref):
    ya = _mm(oa_ref[...], pa_ref[...])
    yr = _mm(or_ref[...], pr_ref[...])
    merged = ga_ref[...].astype(F32) * ya + gb_ref[...].astype(F32) * yr
    o_ref[...] = merged.astype(o_ref.dtype)


def merge(o_a, o_r, p_a, p_r, layer, gates, out_dtype, tm=1024, tn=512):
    t, wa = o_a.shape
    wr = o_r.shape[1]
    d = p_a.shape[2]
    nb = d // tn
    return pl.pallas_call(
        _merge_kernel,
        grid=(t // tm, nb),
        in_specs=[pl.BlockSpec((tm, wa), lambda i, j: (i, 0)),
                  pl.BlockSpec((tm, wr), lambda i, j: (i, 0)),
                  pl.BlockSpec((None, wa, tn), lambda i, j: (layer, 0, j)),
                  pl.BlockSpec((None, wr, tn), lambda i, j: (layer, 0, j)),
                  pl.BlockSpec((tm, tn), lambda i, j: (i, j)),
                  pl.BlockSpec((tm, tn), lambda i, j: (i, nb + j))],
        out_specs=pl.BlockSpec((tm, tn), lambda i, j: (i, j)),
        out_shape=jax.ShapeDtypeStruct((t, d), out_dtype),
        compiler_params=_params("parallel", "arbitrary"),
        name="gated_merge",
    )(o_a, o_r, p_a, p_r, gates, gates)


def _rope_tables(seq, dim):
    inv_freq = 1.0 / (ROPE_THETA ** (jnp.arange(0, dim, 2, dtype=F32) / dim))
    ang = jnp.arange(seq, dtype=F32)[:, None] * inv_freq[None, :]
    return jnp.cos(ang), jnp.sin(ang)


def kernel(x, ffn1_norm, ffn1_w_gate, ffn1_w_up, ffn1_w_down, mix_norm, w_in, w_proj_attn, w_proj_ret,
           w_out, ffn2_norm, ffn2_w_gate, ffn2_w_up, ffn2_w_down, final_norm):
    b, s, d = x.shape
    t = b * s
    depth = w_in.shape[0]
    cos_a, sin_a = _rope_tables(s, HEAD_DIM)
    tab_a = (jnp.concatenate([cos_a, cos_a], axis=1), jnp.concatenate([-sin_a, sin_a], axis=1))
    tab_r = _rope_tables(s, RET_QK_DIM)
    ret_tabs = _ret_tables()
    off = IN_OFFSETS
    qk_scale = jnp.concatenate([jnp.full((1, ATTN_WIDTH), HEAD_DIM ** -0.5, F32), jnp.ones((1, ATTN_WIDTH), F32)],
                               axis=1)
    bf = lambda w: w.astype(BF16)
    f1 = (bf(ffn1_w_gate), bf(ffn1_w_up), bf(ffn1_w_down))
    f2 = (bf(ffn2_w_gate), bf(ffn2_w_up), bf(ffn2_w_down))
    p_a, p_r, w_o = bf(w_proj_attn), bf(w_proj_ret), bf(w_out)

    x = x.reshape(t, d)
    for layer in range(depth):
        x, h = ffn(x, ffn1_norm[layer], *f1, layer, post_gain=mix_norm[layer], post_dtype=BF16)

        def pj(seg, width, mode, dtype, **kw):
            return proj(h, w_in, layer, off[seg], width, mode, dtype, seq=s, **kw)

        wide = dict(tm=1024, n_wblocks=2)
        qk_a = pj(0, 2 * ATTN_WIDTH, "rot128", F32, tables=tab_a, scale=qk_scale, **wide)
        v_a = pj(2, ATTN_WIDTH, "none", F32)
        q_r = pj(3, RET_QK_WIDTH, "rot256", BF16, tables=tab_r, **wide)
        k_r = pj(4, RET_QK_WIDTH, "rot256", BF16, tables=tab_r, scale=RET_QK_DIM ** -0.5, **wide)
        v_r = pj(5, RET_V_WIDTH, "none", BF16, **wide)
        gates = pj(7, 2 * D_MODEL, "sigmoid", BF16, **wide)
        o_a = attention(qk_a.reshape(b, s, -1), v_a.reshape(b, s, -1), BF16)
        o_r = retention(q_r.reshape(b, s, -1), k_r.reshape(b, s, -1), v_r.reshape(b, s, -1),
                        h.reshape(b, s, -1), w_in, layer, off[6], ret_tabs, BF16)
        merged = merge(o_a.reshape(t, -1), o_r.reshape(t, -1), p_a, p_r, layer, gates, BF16)
        x = proj(merged, w_o, layer, 0, D_MODEL, "residual", F32, resid=x, tm=512, wb=D_MODEL)
        last = layer == depth - 1
        x = ffn(x, ffn2_norm[layer], *f2, layer, post_gain=final_norm if last else None, post_dtype=F32,
                emit_x=not last)
    return x.reshape(b, s, d)
```

```python
import functools

import jax
import jax.numpy as jnp
import numpy as np
from jax import lax
from jax.experimental import pallas as pl
from jax.experimental.pallas import tpu as pltpu

F32 = jnp.float32
BF16 = jnp.bfloat16

LANES = 128
VMEM_LIMIT_BYTES = 56 * 1024 * 1024
FFN_VMEM_LIMIT_BYTES = 60 * 1024 * 1024

D_MODEL = 2048
HEAD_DIM = 128
ATTN_GROUPS = ((128, 1), (512, 4), (2048, 16))
ATTN_HEADS_PER_GROUP = 4
N_ATTN_HEADS = len(ATTN_GROUPS) * ATTN_HEADS_PER_GROUP
ATTN_WIDTH = N_ATTN_HEADS * HEAD_DIM
ATTN_OUT_WIDTH = ATTN_HEADS_PER_GROUP * HEAD_DIM
BLOCK = 128
ROPE_THETA = 10000.0
NEG_INF = -1e30
N_RET_HEADS = D_MODEL // 256
RET_QK_DIM = 256
RET_V_DIM = 512
RET_QK_WIDTH = N_RET_HEADS * RET_QK_DIM
RET_V_WIDTH = N_RET_HEADS * RET_V_DIM
GN_EPS = 1e-5
RMS_EPS = 1e-6
IN_SPLITS = (ATTN_WIDTH, ATTN_WIDTH, ATTN_WIDTH, RET_QK_WIDTH, RET_QK_WIDTH,
             RET_V_WIDTH, RET_V_WIDTH, D_MODEL, D_MODEL)
IN_OFFSETS = tuple(int(o) for o in np.cumsum((0,) + IN_SPLITS[:-1]))

ATTN_SUPER = BLOCK * ATTN_GROUPS[-1][1]
ATTN_UNROLL = 8
RET_CHUNK = 256
RET_SLAB = 32


def _params(*sem, vmem_limit_bytes=VMEM_LIMIT_BYTES):
    return pltpu.CompilerParams(dimension_semantics=sem, vmem_limit_bytes=vmem_limit_bytes)


def _mm(a, b):
    return jnp.dot(a, b, preferred_element_type=F32)


def _sigmoid(x):
    return 0.5 * jnp.tanh(0.5 * x) + 0.5


def _mm_nt(a, b):
    return lax.dot_general(a, b, (((1,), (1,)), ((), ())), preferred_element_type=F32)


def _rms(x, g):
    return x * lax.rsqrt(jnp.mean(x * x, axis=-1, keepdims=True) + RMS_EPS) * g


def _ffn_kernel(x_ref, g_ref, wg_ref, wu_ref, wd_ref, *rest, emit_x, emit_norm, h_in_output):
    if emit_norm:
        g2_ref, rest = rest[0], rest[1:]
    outs, h_ref = (rest, rest[-1]) if h_in_output else (rest[:-1], rest[-1])
    y_ref = outs[0]
    f = pl.program_id(1)

    @pl.when(f == 0)
    def _():
        x = x_ref[...]
        h_ref[...] = _rms(x, g_ref[...]).astype(BF16)
        y_ref[...] = x

    h = h_ref[...]
    gate = _mm(h, wg_ref[...])
    up = _mm(h, wu_ref[...])
    act = gate * jax.nn.sigmoid(gate) * up
    y_ref[...] += 0.5 * _mm(act.astype(BF16), wd_ref[...])

    if emit_norm:
        @pl.when(f == pl.num_programs(1) - 1)
        def _():
            outs[-1][...] = _rms(y_ref[...], g2_ref[...]).astype(outs[-1].dtype)


def ffn(x, g, w_gate, w_up, w_down, layer, post_gain=None, post_dtype=None, emit_x=True, tm=1024, tf=512):
    t, d = x.shape
    n_chunks = w_gate.shape[2] // tf
    emit_norm = post_gain is not None
    assert emit_x or post_dtype == F32
    row_spec = pl.BlockSpec((tm, d), lambda i, f: (i, 0))
    vec_spec = pl.BlockSpec((1, d), lambda i, f: (0, 0))
    in_specs = [row_spec, vec_spec,
                pl.BlockSpec((None, d, tf), lambda i, f: (layer, 0, f)),
                pl.BlockSpec((None, d, tf), lambda i, f: (layer, 0, f)),
                pl.BlockSpec((None, tf, d), lambda i, f: (layer, f, 0))]
    args = [x, g.reshape(1, d), w_gate, w_up, w_down]
    out_specs, out_shape = [], []
    if emit_x:
        out_specs.append(row_spec)
        out_shape.append(jax.ShapeDtypeStruct((t, d), F32))
    if emit_norm:
        in_specs.append(vec_spec)
        args.append(post_gain.reshape(1, d))
        out_specs.append(row_spec)
        out_shape.append(jax.ShapeDtypeStruct((t, d), post_dtype))
    h_in_output = emit_x and emit_norm and post_dtype == BF16
    res = pl.pallas_call(
        functools.partial(_ffn_kernel, emit_x=emit_x, emit_norm=emit_norm, h_in_output=h_in_output),
        grid=(t // tm, n_chunks),
        in_specs=in_specs,
        out_specs=out_specs,
        out_shape=out_shape,
        scratch_shapes=[] if h_in_output else [pltpu.VMEM((tm, d), BF16)],
        compiler_params=_params("parallel", "arbitrary", vmem_limit_bytes=FFN_VMEM_LIMIT_BYTES),
        name="ffn",
    )(*args)
    return res if len(res) > 1 else res[0]


def _proj_kernel(a_ref, *refs, mode, scale, tn, n_wblocks):
    w_refs, refs = refs[:n_wblocks], refs[n_wblocks:]
    if scale == "row":
        scale, refs = refs[0][...], refs[1:]
    if mode in ("rot128", "rot256"):
        c_ref, s_ref, o_ref = refs
    elif mode == "residual":
        x_ref, o_ref = refs
    else:
        (o_ref,) = refs
    a = a_ref[...]
    y = jnp.concatenate([_mm(a, w_ref[...].astype(BF16)) for w_ref in w_refs], axis=1)
    if scale is not None:
        y = y * scale
    if mode == "rot128":
        c, s = c_ref[...], s_ref[...]
        for b in range(tn // LANES):
            yb = y[:, b * LANES:(b + 1) * LANES]
            ob = yb * c + pltpu.roll(yb, LANES // 2, axis=1) * s
            o_ref[:, b * LANES:(b + 1) * LANES] = ob.astype(o_ref.dtype)
    elif mode == "rot256":
        c, s = c_ref[...], s_ref[...]
        for b in range(tn // (2 * LANES)):
            y1 = y[:, (2 * b) * LANES:(2 * b + 1) * LANES]
            y2 = y[:, (2 * b + 1) * LANES:(2 * b + 2) * LANES]
            o_ref[:, (2 * b) * LANES:(2 * b + 1) * LANES] = (y1 * c - y2 * s).astype(o_ref.dtype)
            o_ref[:, (2 * b + 1) * LANES:(2 * b + 2) * LANES] = (y2 * c + y1 * s).astype(o_ref.dtype)
    elif mode == "sigmoid":
        o_ref[...] = _sigmoid(y).astype(o_ref.dtype)
    elif mode == "residual":
        o_ref[...] = (x_ref[...] + y).astype(o_ref.dtype)
    else:
        o_ref[...] = y.astype(o_ref.dtype)


def proj(a, w, layer, col_off, n_cols, mode, out_dtype, *, tables=None, scale=None, resid=None, seq=None,
         tm=2048, wb=512, n_wblocks=1):
    t, k = a.shape
    tn = wb * n_wblocks
    jb = col_off // wb
    assert col_off % wb == 0 and n_cols % tn == 0
    in_specs = [pl.BlockSpec((tm, k), lambda i, j: (i, 0))]
    in_specs += [pl.BlockSpec((None, k, wb), lambda i, j, u=u: (layer, 0, jb + j * n_wblocks + u))
                 for u in range(n_wblocks)]
    args = [a] + [w] * n_wblocks
    if scale is not None and not isinstance(scale, float):
        in_specs.append(pl.BlockSpec((1, tn), lambda i, j: (0, j)))
        args.append(scale)
        scale = "row"
    if mode in ("rot128", "rot256"):
        nsb = seq // tm
        tab_spec = pl.BlockSpec((tm, LANES), lambda i, j: (i % nsb, 0))
        in_specs += [tab_spec, tab_spec]
        args += list(tables)
    elif mode == "residual":
        in_specs.append(pl.BlockSpec((tm, tn), lambda i, j: (i, j)))
        args.append(resid)
    return pl.pallas_call(
        functools.partial(_proj_kernel, mode=mode, scale=scale, tn=tn, n_wblocks=n_wblocks),
        grid=(t // tm, n_cols // tn),
        in_specs=in_specs,
        out_specs=pl.BlockSpec((tm, tn), lambda i, j: (i, j)),
        out_shape=jax.ShapeDtypeStruct((t, n_cols), out_dtype),
        compiler_params=_params("parallel", "arbitrary"),
        name="proj_" + mode,
    )(*args)


def _attn_kernel(q0, q1, q2, k0, k1, k2, v0, v1, v2, kp0, kp1, kp2, vp0, vp1, vp2,
                 o_ref, a0, a1, a2, m0, m1, m2, l0, l1, l2):
    row = lax.broadcasted_iota(jnp.int32, (BLOCK, 2 * BLOCK), 0)
    col = lax.broadcasted_iota(jnp.int32, (BLOCK, 2 * BLOCK), 1)
    band = jnp.logical_and(col >= row, col <= row + BLOCK)
    band_halo = jnp.logical_and(band, col >= jnp.where(pl.program_id(2) > 0, 0, BLOCK))
    ones = jnp.ones((2 * BLOCK, LANES), BF16)

    def tiles(items, a_ref, m_ref, l_ref):
        s = [jnp.where(mask, _mm_nt(q.astype(BF16), jnp.concatenate([kp, kc], axis=0).astype(BF16)), NEG_INF)
             for _, q, kp, kc, _, _, mask in items]
        m = [jnp.max(jnp.maximum(si[:, :BLOCK], si[:, BLOCK:]), axis=1, keepdims=True) for si in s]
        p = [jnp.where(it[6], jnp.exp(si - mi), 0.0).astype(BF16) for it, si, mi in zip(items, s, m)]
        al = [_mm(pi, jnp.concatenate([jnp.concatenate([vp, vc], axis=0).astype(BF16), ones], axis=1))
              for (_, _, _, _, vp, vc, _), pi in zip(items, p)]
        for (rows, *_), ali, mi in zip(items, al, m):
            a_ref[rows, :] = ali[:, :LANES]
            l_ref[rows, :] = ali[:, LANES:]
            m_ref[rows, :] = jnp.broadcast_to(mi, (BLOCK, LANES))

    def group(q, k, v, kp, vp, a_ref, m_ref, l_ref, d):
        span = BLOCK * d
        n_sb = ATTN_SUPER // span
        u = ATTN_UNROLL

        def rows_at(start):
            return pl.ds(start, BLOCK, stride=d) if d > 1 else pl.ds(start, BLOCK)

        def item(start, halo):
            rows = rows_at(start)
            if halo:
                return (rows, q[rows, :], kp[rows, :], k[rows, :], vp[rows, :], v[rows, :], band_halo)
            prev = rows_at(start - span)
            return (rows, q[rows, :], k[prev, :], k[rows, :], v[prev, :], v[rows, :], band)

        def batch(items):
            tiles(items, a_ref, m_ref, l_ref)

        if d >= u:
            per = d // u

            def halo_batch(i, carry):
                batch([item(i * u + j, True) for j in range(u)])
                return carry

            lax.fori_loop(0, per, halo_batch, 0)

            def later(sb, carry):
                for rb in range(per):
                    batch([item(sb * span + rb * u + j, False) for j in range(u)])
                return carry

            if n_sb > 1:
                lax.fori_loop(1, n_sb, later, 0)
        else:
            spb = u // d
            batch([item((j // d) * span + j % d, j < d) for j in range(u)])

            def later(i, carry):
                base = pl.multiple_of(i * (spb * span), spb * span)
                batch([item(base + (j // d) * span + j % d, False) for j in range(u)])
                return carry

            if n_sb > spb:
                lax.fori_loop(1, n_sb // spb, later, 0)

    group(q0, k0, v0, kp0, vp0, a0, m0, l0, ATTN_GROUPS[0][1])
    group(q1, k1, v1, kp1, vp1, a1, m1, l1, ATTN_GROUPS[1][1])
    group(q2, k2, v2, kp2, vp2, a2, m2, l2, ATTN_GROUPS[2][1])

    rc = 256

    def merge(i, carry):
        rows = pl.ds(pl.multiple_of(i * rc, rc), rc)
        ma, mb, mc = m0[rows, :], m1[rows, :], m2[rows, :]
        mx = jnp.maximum(jnp.maximum(ma, mb), mc)
        ea, eb, ec = jnp.exp(ma - mx), jnp.exp(mb - mx), jnp.exp(mc - mx)
        num = ea * a0[rows, :] + eb * a1[rows, :] + ec * a2[rows, :]
        den = ea * l0[rows, :] + eb * l1[rows, :] + ec * l2[rows, :]
        o_ref[rows, :] = (num / den).astype(o_ref.dtype)
        return carry

    lax.fori_loop(0, ATTN_SUPER // rc, merge, 0)


def attention(qk, v, out_dtype):
    b, s, _ = v.shape
    hg = ATTN_HEADS_PER_GROUP
    nsb = s // ATTN_SUPER

    def cur(g, col0):
        return pl.BlockSpec((None, ATTN_SUPER, HEAD_DIM), lambda bi, h, j: (bi, j, col0 + g * hg + h))

    def prev(g, col0):
        span = BLOCK * ATTN_GROUPS[g][1]
        per = ATTN_SUPER // span
        return pl.BlockSpec((None, span, HEAD_DIM),
                            lambda bi, h, j: (bi, jnp.maximum(j * per - 1, 0), col0 + g * hg + h))

    groups, k0 = range(len(ATTN_GROUPS)), N_ATTN_HEADS
    in_specs = ([cur(g, 0) for g in groups] + [cur(g, k0) for g in groups] + [cur(g, 0) for g in groups]
                + [prev(g, k0) for g in groups] + [prev(g, 0) for g in groups])
    scratch = [pltpu.VMEM((ATTN_SUPER, LANES), F32) for _ in range(9)]
    return pl.pallas_call(
        _attn_kernel,
        grid=(b, hg, nsb),
        in_specs=in_specs,
        out_specs=pl.BlockSpec((None, ATTN_SUPER, HEAD_DIM), lambda bi, h, j: (bi, j, h)),
        out_shape=jax.ShapeDtypeStruct((b, s, ATTN_OUT_WIDTH), out_dtype),
        scratch_shapes=scratch,
        compiler_params=_params("parallel", "parallel", "arbitrary"),
        name="dilated_attn",
    )(*([qk] * 6 + [v] * 3 + [qk] * 3 + [v] * 3))


def _ret_kernel(q_ref, k_ref, v_ref, hn_ref, wg_ref, dm_ref, qd_ref, kd_ref, cd_ref, o_ref,
                state_ref, kv_ref, sbf_ref, gate_ref, *, n_chunks):
    @pl.when(pl.program_id(2) == 0)
    def _():
        state_ref[...] = jnp.zeros_like(state_ref)

    c_len = RET_CHUNK
    rows = [pl.ds(c * c_len, c_len) for c in range(n_chunks)]

    dmask, qdec, kdec, cdec = dm_ref[...], qd_ref[...], kd_ref[...], cd_ref[...]
    kd_t = [(k_ref[r, :].astype(F32) * kdec).T.astype(BF16) for r in rows]
    for c, r in enumerate(rows):
        kv_ref[c] = _mm(kd_t[c], v_ref[r, :])
    scores = [_mm_nt(q_ref[r, :], k_ref[r, :]) for r in rows]
    gate_ref[...] = _mm(hn_ref[...], wg_ref[...].astype(BF16))

    for i in range(RET_QK_DIM // RET_SLAB):
        rs = pl.ds(i * RET_SLAB, RET_SLAB)
        st = state_ref[rs, :]
        for c in range(n_chunks):
            sbf_ref[c, rs, :] = st.astype(BF16)
            st = st * cdec + kv_ref[c, rs, :]
        state_ref[rs, :] = st

    for c, r in enumerate(rows):
        lhs = jnp.concatenate([(scores[c] * dmask).astype(BF16),
                               (q_ref[r, :].astype(F32) * qdec).astype(BF16)], axis=1)
        rhs = jnp.concatenate([v_ref[r, :], sbf_ref[c]], axis=0)
        o = _mm(lhs, rhs)
        mu = jnp.mean(o, axis=-1, keepdims=True)
        oc = o - mu
        var = jnp.mean(oc * oc, axis=-1, keepdims=True)
        on = oc * lax.rsqrt(var + GN_EPS)
        g = gate_ref[r, :]
        o_ref[r, :] = (on * (g * _sigmoid(g))).astype(o_ref.dtype)


def _ret_tables():
    c_len = RET_CHUNK
    log_gamma = jnp.log1p(-jnp.exp2(-5.0 - jnp.arange(N_RET_HEADS, dtype=F32)))
    idx = jnp.arange(c_len, dtype=F32)
    rel = idx[:, None] - idx[None, :]
    dmask = jnp.where(rel >= 0, jnp.exp(log_gamma[:, None, None] * jnp.maximum(rel, 0.0)), 0.0)
    qdec = jnp.exp(log_gamma[:, None] * (idx + 1.0)[None, :])
    kdec = jnp.exp(log_gamma[:, None] * (c_len - 1.0 - idx)[None, :])
    cdec = jnp.exp(log_gamma * c_len)
    qdec = jnp.broadcast_to(qdec[:, :, None], (N_RET_HEADS, c_len, RET_QK_DIM))
    kdec = jnp.broadcast_to(kdec[:, :, None], (N_RET_HEADS, c_len, RET_QK_DIM))
    cdec = jnp.broadcast_to(cdec[:, None, None], (N_RET_HEADS, 1, RET_V_DIM))
    return dmask, qdec, kdec, cdec


def retention(q, k, v, hn, w_in, layer, gate_off, tables, out_dtype, tc=2048):
    b, s, _ = q.shape
    d = hn.shape[2]
    n_chunks = tc // RET_CHUNK
    gate_blk = gate_off // RET_V_DIM
    assert gate_off % RET_V_DIM == 0
    qk_spec = pl.BlockSpec((None, tc, RET_QK_DIM), lambda bi, h, j: (bi, j, h))
    v_spec = pl.BlockSpec((None, tc, RET_V_DIM), lambda bi, h, j: (bi, j, h))
    hn_spec = pl.BlockSpec((None, tc, d), lambda bi, h, j: (bi, j, 0))
    wg_spec = pl.BlockSpec((None, d, RET_V_DIM), lambda bi, h, j: (layer, 0, gate_blk + h))

    def tab(shape):
        return pl.BlockSpec((None,) + shape, lambda bi, h, j: (h, 0, 0))

    return pl.pallas_call(
        functools.partial(_ret_kernel, n_chunks=n_chunks),
        grid=(b, N_RET_HEADS, s // tc),
        in_specs=[qk_spec, qk_spec, v_spec, hn_spec, wg_spec,
                  tab((RET_CHUNK, RET_CHUNK)), tab((RET_CHUNK, RET_QK_DIM)), tab((RET_CHUNK, RET_QK_DIM)),
                  tab((1, RET_V_DIM))],
        out_specs=v_spec,
        out_shape=jax.ShapeDtypeStruct((b, s, RET_V_WIDTH), out_dtype),
        scratch_shapes=[pltpu.VMEM((RET_QK_DIM, RET_V_DIM), F32),
                        pltpu.VMEM((n_chunks, RET_QK_DIM, RET_V_DIM), F32),
                        pltpu.VMEM((n_chunks, RET_QK_DIM, RET_V_DIM), BF16),
                        pltpu.VMEM((tc, RET_V_DIM), F32)],
        compiler_params=_params("parallel", "parallel", "arbitrary"),
        name="retention",
    )(q, k, v, hn, w_in, *tables)


def _merge_kernel(oa_ref, or_ref, pa_ref, pr_ref, ga_ref, gb_ref, o_ref):
    ya = _mm(oa_ref[...], pa_ref[...])
    yr = _mm(or_ref[...], pr_ref[...])
    merged = ga_ref[...].astype(F32) * ya + gb_ref[...].astype(F32) * yr
    o_ref[...] = merged.astype(o_ref.dtype)


def merge(o_a, o_r, p_a, p_r, layer, gates, out_dtype, tm=1024, tn=512):
    t, wa = o_a.shape
    wr = o_r.shape[1]
    d = p_a.shape[2]
    nb = d // tn
    return pl.pallas_call(
        _merge_kernel,
        grid=(t // tm, nb),
        in_specs=[pl.BlockSpec((tm, wa), lambda i, j: (i, 0)),
                  pl.BlockSpec((tm, wr), lambda i, j: (i, 0)),
                  pl.BlockSpec((None, wa, tn), lambda i, j: (layer, 0, j)),
                  pl.BlockSpec((None, wr, tn), lambda i, j: (layer, 0, j)),
                  pl.BlockSpec((tm, tn), lambda i, j: (i, j)),
                  pl.BlockSpec((tm, tn), lambda i, j: (i, nb + j))],
        out_specs=pl.BlockSpec((tm, tn), lambda i, j: (i, j)),
        out_shape=jax.ShapeDtypeStruct((t, d), out_dtype),
        compiler_params=_params("parallel", "arbitrary"),
        name="gated_merge",
    )(o_a, o_r, p_a, p_r, gates, gates)


def _rope_tables(seq, dim):
    inv_freq = 1.0 / (ROPE_THETA ** (jnp.arange(0, dim, 2, dtype=F32) / dim))
    ang = jnp.arange(seq, dtype=F32)[:, None] * inv_freq[None, :]
    return jnp.cos(ang), jnp.sin(ang)


def kernel(x, ffn1_norm, ffn1_w_gate, ffn1_w_up, ffn1_w_down, mix_norm, w_in, w_proj_attn, w_proj_ret,
           w_out, ffn2_norm, ffn2_w_gate, ffn2_w_up, ffn2_w_down, final_norm):
    b, s, d = x.shape
    t = b * s
    depth = w_in.shape[0]
    cos_a, sin_a = _rope_tables(s, HEAD_DIM)
    tab_a = (jnp.concatenate([cos_a, cos_a], axis=1), jnp.concatenate([-sin_a, sin_a], axis=1))
    tab_r = _rope_tables(s, RET_QK_DIM)
    ret_tabs = _ret_tables()
    off = IN_OFFSETS
    qk_scale = jnp.concatenate([jnp.full((1, ATTN_WIDTH), HEAD_DIM ** -0.5, F32), jnp.ones((1, ATTN_WIDTH), F32)],
                               axis=1)
    bf = lambda w: w.astype(BF16)
    f1 = (bf(ffn1_w_gate), bf(ffn1_w_up), bf(ffn1_w_down))
    f2 = (bf(ffn2_w_gate), bf(ffn2_w_up), bf(ffn2_w_down))
    p_a, p_r, w_o = bf(w_proj_attn), bf(w_proj_ret), bf(w_out)

    x = x.reshape(t, d)
    for layer in range(depth):
        x, h = ffn(x, ffn1_norm[layer], *f1, layer, post_gain=mix_norm[layer], post_dtype=BF16)

        def pj(seg, width, mode, dtype, **kw):
            return proj(h, w_in, layer, off[seg], width, mode, dtype, seq=s, **kw)

        wide = dict(tm=1024, n_wblocks=2)
        qk_a = pj(0, 2 * ATTN_WIDTH, "rot128", F32, tables=tab_a, scale=qk_scale, **wide)
        v_a = pj(2, ATTN_WIDTH, "none", F32)
        q_r = pj(3, RET_QK_WIDTH, "rot256", BF16, tables=tab_r, **wide)
        k_r = pj(4, RET_QK_WIDTH, "rot256", BF16, tables=tab_r, scale=RET_QK_DIM ** -0.5, **wide)
        v_r = pj(5, RET_V_WIDTH, "none", BF16, **wide)
        gates = pj(7, 2 * D_MODEL, "sigmoid", BF16, **wide)
        o_a = attention(qk_a.reshape(b, s, -1), v_a.reshape(b, s, -1), BF16)
        o_r = retention(q_r.reshape(b, s, -1), k_r.reshape(b, s, -1), v_r.reshape(b, s, -1),
                        h.reshape(b, s, -1), w_in, layer, off[6], ret_tabs, BF16)
        merged = merge(o_a.reshape(t, -1), o_r.reshape(t, -1), p_a, p_r, layer, gates, BF16)
        x = proj(merged, w_o, layer, 0, D_MODEL, "residual", F32, resid=x, tm=512, wb=D_MODEL)
        last = layer == depth - 1
        x = ffn(x, ffn2_norm[layer], *f2, layer, post_gain=final_norm if last else None, post_dtype=F32,
                emit_x=not last)
    return x.reshape(b, s, d)
```

```python
import functools

import jax
import jax.numpy as jnp
import numpy as np
from jax import lax
from jax.experimental import pallas as pl
from jax.experimental.pallas import tpu as pltpu

F32 = jnp.float32
BF16 = jnp.bfloat16

LANES = 128
VMEM_LIMIT_BYTES = 56 * 1024 * 1024
FFN_VMEM_LIMIT_BYTES = 60 * 1024 * 1024

D_MODEL = 2048
HEAD_DIM = 128
ATTN_GROUPS = ((128, 1), (512, 4), (2048, 16))
ATTN_HEADS_PER_GROUP = 4
N_ATTN_HEADS = len(ATTN_GROUPS) * ATTN_HEADS_PER_GROUP
ATTN_WIDTH = N_ATTN_HEADS * HEAD_DIM
ATTN_OUT_WIDTH = ATTN_HEADS_PER_GROUP * HEAD_DIM
BLOCK = 128
ROPE_THETA = 10000.0
NEG_INF = -1e30
N_RET_HEADS = D_MODEL // 256
RET_QK_DIM = 256
RET_V_DIM = 512
RET_QK_WIDTH = N_RET_HEADS * RET_QK_DIM
RET_V_WIDTH = N_RET_HEADS * RET_V_DIM
GN_EPS = 1e-5
RMS_EPS = 1e-6
IN_SPLITS = (ATTN_WIDTH, ATTN_WIDTH, ATTN_WIDTH, RET_QK_WIDTH, RET_QK_WIDTH,
             RET_V_WIDTH, RET_V_WIDTH, D_MODEL, D_MODEL)
IN_OFFSETS = tuple(int(o) for o in np.cumsum((0,) + IN_SPLITS[:-1]))

ATTN_SUPER = BLOCK * ATTN_GROUPS[-1][1]
ATTN_UNROLL = 8
RET_CHUNK = 256
RET_SLAB = 32


def _params(*sem, vmem_limit_bytes=VMEM_LIMIT_BYTES):
    return pltpu.CompilerParams(dimension_semantics=sem, vmem_limit_bytes=vmem_limit_bytes)


def _mm(a, b):
    return jnp.dot(a, b, preferred_element_type=F32)


def _sigmoid(x):
    return 0.5 * jnp.tanh(0.5 * x) + 0.5


def _mm_nt(a, b):
    return lax.dot_general(a, b, (((1,), (1,)), ((), ())), preferred_element_type=F32)


def _rms(x, g):
    return x * lax.rsqrt(jnp.mean(x * x, axis=-1, keepdims=True) + RMS_EPS) * g


def _ffn_kernel(x_ref, g_ref, wg_ref, wu_ref, wd_ref, *rest, emit_x, emit_norm, h_in_output):
    if emit_norm:
        g2_ref, rest = rest[0], rest[1:]
    outs, h_ref = (rest, rest[-1]) if h_in_output else (rest[:-1], rest[-1])
    y_ref = outs[0]
    f = pl.program_id(1)

    @pl.when(f == 0)
    def _():
        x = x_ref[...]
        h_ref[...] = _rms(x, g_ref[...]).astype(BF16)
        y_ref[...] = x

    h = h_ref[...]
    gate = _mm(h, wg_ref[...])
    up = _mm(h, wu_ref[...])
    act = gate * jax.nn.sigmoid(gate) * up
    y_ref[...] += 0.5 * _mm(act.astype(BF16), wd_ref[...])

    if emit_norm:
        @pl.when(f == pl.num_programs(1) - 1)
        def _():
            outs[-1][...] = _rms(y_ref[...], g2_ref[...]).astype(outs[-1].dtype)


def ffn(x, g, w_gate, w_up, w_down, layer, post_gain=None, post_dtype=None, emit_x=True, tm=1024, tf=512):
    t, d = x.shape
    n_chunks = w_gate.shape[2] // tf
    emit_norm = post_gain is not None
    assert emit_x or post_dtype == F32
    row_spec = pl.BlockSpec((tm, d), lambda i, f: (i, 0))
    vec_spec = pl.BlockSpec((1, d), lambda i, f: (0, 0))
    in_specs = [row_spec, vec_spec,
                pl.BlockSpec((None, d, tf), lambda i, f: (layer, 0, f)),
                pl.BlockSpec((None, d, tf), lambda i, f: (layer, 0, f)),
                pl.BlockSpec((None, tf, d), lambda i, f: (layer, f, 0))]
    args = [x, g.reshape(1, d), w_gate, w_up, w_down]
    out_specs, out_shape = [], []
    if emit_x:
        out_specs.append(row_spec)
        out_shape.append(jax.ShapeDtypeStruct((t, d), F32))
    if emit_norm:
        in_specs.append(vec_spec)
        args.append(post_gain.reshape(1, d))
        out_specs.append(row_spec)
        out_shape.append(jax.ShapeDtypeStruct((t, d), post_dtype))
    h_in_output = emit_x and emit_norm and post_dtype == BF16
    res = pl.pallas_call(
        functools.partial(_ffn_kernel, emit_x=emit_x, emit_norm=emit_norm, h_in_output=h_in_output),
        grid=(t // tm, n_chunks),
        in_specs=in_specs,
        out_specs=out_specs,
        out_shape=out_shape,
        scratch_shapes=[] if h_in_output else [pltpu.VMEM((tm, d), BF16)],
        compiler_params=_params("parallel", "arbitrary", vmem_limit_bytes=FFN_VMEM_LIMIT_BYTES),
        name="ffn",
    )(*args)
    return res if len(res) > 1 else res[0]


def _proj_kernel(a_ref, *refs, mode, scale, tn, n_wblocks):
    w_refs, refs = refs[:n_wblocks], refs[n_wblocks:]
    if scale == "row":
        scale, refs = refs[0][...], refs[1:]
    if mode in ("rot128", "rot256"):
        c_ref, s_ref, o_ref = refs
    elif mode == "residual":
        x_ref, o_ref = refs
    else:
        (o_ref,) = refs
    a = a_ref[...]
    y = jnp.concatenate([_mm(a, w_ref[...].astype(BF16)) for w_ref in w_refs], axis=1)
    if scale is not None:
        y = y * scale
    if mode == "rot128":
        c, s = c_ref[...], s_ref[...]
        for b in range(tn // LANES):
            yb = y[:, b * LANES:(b + 1) * LANES]
            ob = yb * c + pltpu.roll(yb, LANES // 2, axis=1) * s
            o_ref[:, b * LANES:(b + 1) * LANES] = ob.astype(o_ref.dtype)
    elif mode == "rot256":
        c, s = c_ref[...], s_ref[...]
        for b in range(tn // (2 * LANES)):
            y1 = y[:, (2 * b) * LANES:(2 * b + 1) * LANES]
            y2 = y[:, (2 * b + 1) * LANES:(2 * b + 2) * LANES]
            o_ref[:, (2 * b) * LANES:(2 * b + 1) * LANES] = (y1 * c - y2 * s).astype(o_ref.dtype)
            o_ref[:, (2 * b + 1) * LANES:(2 * b + 2) * LANES] = (y2 * c + y1 * s).astype(o_ref.dtype)
    elif mode == "sigmoid":
        o_ref[...] = _sigmoid(y).astype(o_ref.dtype)
    elif mode == "residual":
        o_ref[...] = (x_ref[...] + y).astype(o_ref.dtype)
    else:
        o_ref[...] = y.astype(o_ref.dtype)


def proj(a, w, layer, col_off, n_cols, mode, out_dtype, *, tables=None, scale=None, resid=None, seq=None,
         tm=2048, wb=512, n_wblocks=1):
    t, k = a.shape
    tn = wb * n_wblocks
    jb = col_off // wb
    assert col_off % wb == 0 and n_cols % tn == 0
    in_specs = [pl.BlockSpec((tm, k), lambda i, j: (i, 0))]
    in_specs += [pl.BlockSpec((None, k, wb), lambda i, j, u=u: (layer, 0, jb + j * n_wblocks + u))
                 for u in range(n_wblocks)]
    args = [a] + [w] * n_wblocks
    if scale is not None and not isinstance(scale, float):
        in_specs.append(pl.BlockSpec((1, tn), lambda i, j: (0, j)))
        args.append(scale)
        scale = "row"
    if mode in ("rot128", "rot256"):
        nsb = seq // tm
        tab_spec = pl.BlockSpec((tm, LANES), lambda i, j: (i % nsb, 0))
        in_specs += [tab_spec, tab_spec]
        args += list(tables)
    elif mode == "residual":
        in_specs.append(pl.BlockSpec((tm, tn), lambda i, j: (i, j)))
        args.append(resid)
    return pl.pallas_call(
        functools.partial(_proj_kernel, mode=mode, scale=scale, tn=tn, n_wblocks=n_wblocks),
        grid=(t // tm, n_cols // tn),
        in_specs=in_specs,
        out_specs=pl.BlockSpec((tm, tn), lambda i, j: (i, j)),
        out_shape=jax.ShapeDtypeStruct((t, n_cols), out_dtype),
        compiler_params=_params("parallel", "arbitrary"),
        name="proj_" + mode,
    )(*args)


def _attn_kernel(q0, q1, q2, k0, k1, k2, v0, v1, v2, kp0, kp1, kp2, vp0, vp1, vp2,
                 o_ref, a0, a1, a2, m0, m1, m2, l0, l1, l2):
    row = lax.broadcasted_iota(jnp.int32, (BLOCK, 2 * BLOCK), 0)
    col = lax.broadcasted_iota(jnp.int32, (BLOCK, 2 * BLOCK), 1)
    band = jnp.logical_and(col >= row, col <= row + BLOCK)
    band_halo = jnp.logical_and(band, col >= jnp.where(pl.program_id(2) > 0, 0, BLOCK))
    ones = jnp.ones((2 * BLOCK, LANES), BF16)

    def tiles(items, a_ref, m_ref, l_ref):
        s = [jnp.where(mask, _mm_nt(q.astype(BF16), jnp.concatenate([kp, kc], axis=0).astype(BF16)), NEG_INF)
             for _, q, kp, kc, _, _, mask in items]
        m = [jnp.max(jnp.maximum(si[:, :BLOCK], si[:, BLOCK:]), axis=1, keepdims=True) for si in s]
        p = [jnp.where(it[6], jnp.exp(si - mi), 0.0).astype(BF16) for it, si, mi in zip(items, s, m)]
        al = [_mm(pi, jnp.concatenate([jnp.concatenate([vp, vc], axis=0).astype(BF16), ones], axis=1))
              for (_, _, _, _, vp, vc, _), pi in zip(items, p)]
        for (rows, *_), ali, mi in zip(items, al, m):
            a_ref[rows, :] = ali[:, :LANES]
            l_ref[rows, :] = ali[:, LANES:]
            m_ref[rows, :] = jnp.broadcast_to(mi, (BLOCK, LANES))

    def group(q, k, v, kp, vp, a_ref, m_ref, l_ref, d):
        span = BLOCK * d
        n_sb = ATTN_SUPER // span
        u = ATTN_UNROLL

        def rows_at(start):
            return pl.ds(start, BLOCK, stride=d) if d > 1 else pl.ds(start, BLOCK)

        def item(start, halo):
            rows = rows_at(start)
            if halo:
                return (rows, q[rows, :], kp[rows, :], k[rows, :], vp[rows, :], v[rows, :], band_halo)
            prev = rows_at(start - span)
            return (rows, q[rows, :], k[prev, :], k[rows, :], v[prev, :], v[rows, :], band)

        def batch(items):
            tiles(items, a_ref, m_ref, l_ref)

        if d >= u:
            per = d // u

            def halo_batch(i, carry):
                batch([item(i * u + j, True) for j in range(u)])
                return carry

            lax.fori_loop(0, per, halo_batch, 0)

            def later(sb, carry):
                for rb in range(per):
                    batch([item(sb * span + rb * u + j, False) for j in range(u)])
                return carry

            if n_sb > 1:
                lax.fori_loop(1, n_sb, later, 0)
        else:
            spb = u // d
            batch([item((j // d) * span + j % d, j < d) for j in range(u)])

            def later(i, carry):
                base = pl.multiple_of(i * (spb * span), spb * span)
                batch([item(base + (j // d) * span + j % d, False) for j in range(u)])
                return carry

            if n_sb > spb:
                lax.fori_loop(1, n_sb // spb, later, 0)

    group(q0, k0, v0, kp0, vp0, a0, m0, l0, ATTN_GROUPS[0][1])
    group(q1, k1, v1, kp1, vp1, a1, m1, l1, ATTN_GROUPS[1][1])
    group(q2, k2, v2, kp2, vp2, a2, m2, l2, ATTN_GROUPS[2][1])

    rc = 256

    def merge(i, carry):
        rows = pl.ds(pl.multiple_of(i * rc, rc), rc)
        ma, mb, mc = m0[rows, :], m1[rows, :], m2[rows, :]
        mx = jnp.maximum(jnp.maximum(ma, mb), mc)
        ea, eb, ec = jnp.exp(ma - mx), jnp.exp(mb - mx), jnp.exp(mc - mx)
        num = ea * a0[rows, :] + eb * a1[rows, :] + ec * a2[rows, :]
        den = ea * l0[rows, :] + eb * l1[rows, :] + ec * l2[rows, :]
        o_ref[rows, :] = (num / den).astype(o_ref.dtype)
        return carry

    lax.fori_loop(0, ATTN_SUPER // rc, merge, 0)


def attention(qk, v, out_dtype):
    b, s, _ = v.shape
    hg = ATTN_HEADS_PER_GROUP
    nsb = s // ATTN_SUPER

    def cur(g, col0):
        return pl.BlockSpec((None, ATTN_SUPER, HEAD_DIM), lambda bi, h, j: (bi, j, col0 + g * hg + h))

    def prev(g, col0):
        span = BLOCK * ATTN_GROUPS[g][1]
        per = ATTN_SUPER // span
        return pl.BlockSpec((None, span, HEAD_DIM),
                            lambda bi, h, j: (bi, jnp.maximum(j * per - 1, 0), col0 + g * hg + h))

    groups, k0 = range(len(ATTN_GROUPS)), N_ATTN_HEADS
    in_specs = ([cur(g, 0) for g in groups] + [cur(g, k0) for g in groups] + [cur(g, 0) for g in groups]
                + [prev(g, k0) for g in groups] + [prev(g, 0) for g in groups])
    scratch = [pltpu.VMEM((ATTN_SUPER, LANES), F32) for _ in range(9)]
    return pl.pallas_call(
        _attn_kernel,
        grid=(b, hg, nsb),
        in_specs=in_specs,
        out_specs=pl.BlockSpec((None, ATTN_SUPER, HEAD_DIM), lambda bi, h, j: (bi, j, h)),
        out_shape=jax.ShapeDtypeStruct((b, s, ATTN_OUT_WIDTH), out_dtype),
        scratch_shapes=scratch,
        compiler_params=_params("parallel", "parallel", "arbitrary"),
        name="dilated_attn",
    )(*([qk] * 6 + [v] * 3 + [qk] * 3 + [v] * 3))


def _ret_kernel(q_ref, k_ref, v_ref, hn_ref, wg_ref, dm_ref, qd_ref, kd_ref, cd_ref, o_ref,
                state_ref, kv_ref, sbf_ref, gate_ref, *, n_chunks):
    @pl.when(pl.program_id(2) == 0)
    def _():
        state_ref[...] = jnp.zeros_like(state_ref)

    c_len = RET_CHUNK
    rows = [pl.ds(c * c_len, c_len) for c in range(n_chunks)]

    dmask, qdec, kdec, cdec = dm_ref[...], qd_ref[...], kd_ref[...], cd_ref[...]
    kd_t = [(k_ref[r, :].astype(F32) * kdec).T.astype(BF16) for r in rows]
    for c, r in enumerate(rows):
        kv_ref[c] = _mm(kd_t[c], v_ref[r, :])
    scores = [_mm_nt(q_ref[r, :], k_ref[r, :]) for r in rows]
    gate_ref[...] = _mm(hn_ref[...], wg_ref[...].astype(BF16))

    for i in range(RET_QK_DIM // RET_SLAB):
        rs = pl.ds(i * RET_SLAB, RET_SLAB)
        st = state_ref[rs, :]
        for c in range(n_chunks):
            sbf_ref[c, rs, :] = st.astype(BF16)
            st = st * cdec + kv_ref[c, rs, :]
        state_ref[rs, :] = st

    for c, r in enumerate(rows):
        lhs = jnp.concatenate([(scores[c] * dmask).astype(BF16),
                               (q_ref[r, :].astype(F32) * qdec).astype(BF16)], axis=1)
        rhs = jnp.concatenate([v_ref[r, :], sbf_ref[c]], axis=0)
        o = _mm(lhs, rhs)
        mu = jnp.mean(o, axis=-1, keepdims=True)
        oc = o - mu
        var = jnp.mean(oc * oc, axis=-1, keepdims=True)
        on = oc * lax.rsqrt(var + GN_EPS)
        g = gate_ref[r, :]
        o_ref[r, :] = (on * (g * _sigmoid(g))).astype(o_ref.dtype)


def _ret_tables():
    c_len = RET_CHUNK
    log_gamma = jnp.log1p(-jnp.exp2(-5.0 - jnp.arange(N_RET_HEADS, dtype=F32)))
    idx = jnp.arange(c_len, dtype=F32)
    rel = idx[:, None] - idx[None, :]
    dmask = jnp.where(rel >= 0, jnp.exp(log_gamma[:, None, None] * jnp.maximum(rel, 0.0)), 0.0)
    qdec = jnp.exp(log_gamma[:, None] * (idx + 1.0)[None, :])
    kdec = jnp.exp(log_gamma[:, None] * (c_len - 1.0 - idx)[None, :])
    cdec = jnp.exp(log_gamma * c_len)
    qdec = jnp.broadcast_to(qdec[:, :, None], (N_RET_HEADS, c_len, RET_QK_DIM))
    kdec = jnp.broadcast_to(kdec[:, :, None], (N_RET_HEADS, c_len, RET_QK_DIM))
    cdec = jnp.broadcast_to(cdec[:, None, None], (N_RET_HEADS, 1, RET_V_DIM))
    return dmask, qdec, kdec, cdec


def retention(q, k, v, hn, w_in, layer, gate_off, tables, out_dtype, tc=2048):
    b, s, _ = q.shape
    d = hn.shape[2]
    n_chunks = tc // RET_CHUNK
    gate_blk = gate_off // RET_V_DIM
    assert gate_off % RET_V_DIM == 0
    qk_spec = pl.BlockSpec((None, tc, RET_QK_DIM), lambda bi, h, j: (bi, j, h))
    v_spec = pl.BlockSpec((None, tc, RET_V_DIM), lambda bi, h, j: (bi, j, h))
    hn_spec = pl.BlockSpec((None, tc, d), lambda bi, h, j: (bi, j, 0))
    wg_spec = pl.BlockSpec((None, d, RET_V_DIM), lambda bi, h, j: (layer, 0, gate_blk + h))

    def tab(shape):
        return pl.BlockSpec((None,) + shape, lambda bi, h, j: (h, 0, 0))

    return pl.pallas_call(
        functools.partial(_ret_kernel, n_chunks=n_chunks),
        grid=(b, N_RET_HEADS, s // tc),
        in_specs=[qk_spec, qk_spec, v_spec, hn_spec, wg_spec,
                  tab((RET_CHUNK, RET_CHUNK)), tab((RET_CHUNK, RET_QK_DIM)), tab((RET_CHUNK, RET_QK_DIM)),
                  tab((1, RET_V_DIM))],
        out_specs=v_spec,
        out_shape=jax.ShapeDtypeStruct((b, s, RET_V_WIDTH), out_dtype),
        scratch_shapes=[pltpu.VMEM((RET_QK_DIM, RET_V_DIM), F32),
                        pltpu.VMEM((n_chunks, RET_QK_DIM, RET_V_DIM), F32),
                        pltpu.VMEM((n_chunks, RET_QK_DIM, RET_V_DIM), BF16),
                        pltpu.VMEM((tc, RET_V_DIM), F32)],
        compiler_params=_params("parallel", "parallel", "arbitrary"),
        name="retention",
    )(q, k, v, hn, w_in, *tables)


def _merge_kernel(oa_ref, or_ref, pa_ref, pr_ref, ga_ref, gb_ref, o_ref):
    ya = _mm(oa_ref[...], pa_ref[...])
    yr = _mm(or_ref[...], pr_ref[...])
    merged = ga_ref[...].astype(F32) * ya + gb_ref[...].astype(F32) * yr
    o_ref[...] = merged.astype(o_ref.dtype)


def merge(o_a, o_r, p_a, p_r, layer, gates, out_dtype, tm=1024, tn=512):
    t, wa = o_a.shape
    wr = o_r.shape[1]
    d = p_a.shape[2]
    nb = d // tn
    return pl.pallas_call(
        _merge_kernel,
        grid=(t // tm, nb),
        in_specs=[pl.BlockSpec((tm, wa), lambda i, j: (i, 0)),
                  pl.BlockSpec((tm, wr), lambda i, j: (i, 0)),
                  pl.BlockSpec((None, wa, tn), lambda i, j: (layer, 0, j)),
                  pl.BlockSpec((None, wr, tn), lambda i, j: (layer, 0, j)),
                  pl.BlockSpec((tm, tn), lambda i, j: (i, j)),
                  pl.BlockSpec((tm, tn), lambda i, j: (i, nb + j))],
        out_specs=pl.BlockSpec((tm, tn), lambda i, j: (i, j)),
        out_shape=jax.ShapeDtypeStruct((t, d), out_dtype),
        compiler_params=_params("parallel", "arbitrary"),
        name="gated_merge",
    )(o_a, o_r, p_a, p_r, gates, gates)


def _rope_tables(seq, dim):
    inv_freq = 1.0 / (ROPE_THETA ** (jnp.arange(0, dim, 2, dtype=F32) / dim))
    ang = jnp.arange(seq, dtype=F32)[:, None] * inv_freq[None, :]
    return jnp.cos(ang), jnp.sin(ang)


def kernel(x, ffn1_norm, ffn1_w_gate, ffn1_w_up, ffn1_w_down, mix_norm, w_in, w_proj_attn, w_proj_ret,
           w_out, ffn2_norm, ffn2_w_gate, ffn2_w_up, ffn2_w_down, final_norm):
    b, s, d = x.shape
    t = b * s
    depth = w_in.shape[0]
    cos_a, sin_a = _rope_tables(s, HEAD_DIM)
    tab_a = (jnp.concatenate([cos_a, cos_a], axis=1), jnp.concatenate([-sin_a, sin_a], axis=1))
    tab_r = _rope_tables(s, RET_QK_DIM)
    ret_tabs = _ret_tables()
    off = IN_OFFSETS
    qk_scale = jnp.concatenate([jnp.full((1, ATTN_WIDTH), HEAD_DIM ** -0.5, F32), jnp.ones((1, ATTN_WIDTH), F32)],
                               axis=1)
    bf = lambda w: w.astype(BF16)
    f1 = (bf(ffn1_w_gate), bf(ffn1_w_up), bf(ffn1_w_down))
    f2 = (bf(ffn2_w_gate), bf(ffn2_w_up), bf(ffn2_w_down))
    p_a, p_r, w_o = bf(w_proj_attn), bf(w_proj_ret), bf(w_out)

    x = x.reshape(t, d)
    for layer in range(depth):
        x, h = ffn(x, ffn1_norm[layer], *f1, layer, post_gain=mix_norm[layer], post_dtype=BF16)

        def pj(seg, width, mode, dtype, **kw):
            return proj(h, w_in, layer, off[seg], width, mode, dtype, seq=s, **kw)

        wide, big = dict(tm=1024, n_wblocks=2), dict(tm=2048, n_wblocks=2)
        qk_a = pj(0, 2 * ATTN_WIDTH, "rot128", F32, tables=tab_a, scale=qk_scale, **wide)
        v_a = pj(2, ATTN_WIDTH, "none", F32)
        q_r = pj(3, RET_QK_WIDTH, "rot256", BF16, tables=tab_r, **wide)
        k_r = pj(4, RET_QK_WIDTH, "rot256", BF16, tables=tab_r, scale=RET_QK_DIM ** -0.5, **wide)
        v_r = pj(5, RET_V_WIDTH, "none", BF16, **big)
        gates = pj(7, 2 * D_MODEL, "sigmoid", BF16, **big)
        o_a = attention(qk_a.reshape(b, s, -1), v_a.reshape(b, s, -1), BF16)
        o_r = retention(q_r.reshape(b, s, -1), k_r.reshape(b, s, -1), v_r.reshape(b, s, -1),
                        h.reshape(b, s, -1), w_in, layer, off[6], ret_tabs, BF16)
        merged = merge(o_a.reshape(t, -1), o_r.reshape(t, -1), p_a, p_r, layer, gates, BF16)
        x = proj(merged, w_o, layer, 0, D_MODEL, "residual", F32, resid=x, tm=512, wb=D_MODEL)
        last = layer == depth - 1
        x = ffn(x, ffn2_norm[layer], *f2, layer, post_gain=final_norm if last else None, post_dtype=F32,
                emit_x=not last)
    return x.reshape(b, s, d)
```

```python
import functools

import jax
import jax.numpy as jnp
import numpy as np
from jax import lax
from jax.experimental import pallas as pl
from jax.experimental.pallas import tpu as pltpu

F32 = jnp.float32
BF16 = jnp.bfloat16

LANES = 128
VMEM_LIMIT_BYTES = 56 * 1024 * 1024
FFN_VMEM_LIMIT_BYTES = 60 * 1024 * 1024

D_MODEL = 2048
HEAD_DIM = 128
ATTN_GROUPS = ((128, 1), (512, 4), (2048, 16))
ATTN_HEADS_PER_GROUP = 4
N_ATTN_HEADS = len(ATTN_GROUPS) * ATTN_HEADS_PER_GROUP
ATTN_WIDTH = N_ATTN_HEADS * HEAD_DIM
ATTN_OUT_WIDTH = ATTN_HEADS_PER_GROUP * HEAD_DIM
BLOCK = 128
ROPE_THETA = 10000.0
NEG_INF = -1e30
N_RET_HEADS = D_MODEL // 256
RET_QK_DIM = 256
RET_V_DIM = 512
RET_QK_WIDTH = N_RET_HEADS * RET_QK_DIM
RET_V_WIDTH = N_RET_HEADS * RET_V_DIM
GN_EPS = 1e-5
RMS_EPS = 1e-6
IN_SPLITS = (ATTN_WIDTH, ATTN_WIDTH, ATTN_WIDTH, RET_QK_WIDTH, RET_QK_WIDTH,
             RET_V_WIDTH, RET_V_WIDTH, D_MODEL, D_MODEL)
IN_OFFSETS = tuple(int(o) for o in np.cumsum((0,) + IN_SPLITS[:-1]))

ATTN_SUPER = BLOCK * ATTN_GROUPS[-1][1]
ATTN_UNROLL = 8
RET_CHUNK = 256
RET_SLAB = 32


def _params(*sem, vmem_limit_bytes=VMEM_LIMIT_BYTES):
    return pltpu.CompilerParams(dimension_semantics=sem, vmem_limit_bytes=vmem_limit_bytes)


def _mm(a, b):
    return jnp.dot(a, b, preferred_element_type=F32)


def _sigmoid(x):
    return 0.5 * jnp.tanh(0.5 * x) + 0.5


def _mm_nt(a, b):
    return lax.dot_general(a, b, (((1,), (1,)), ((), ())), preferred_element_type=F32)


def _rms(x, g):
    return x * lax.rsqrt(jnp.mean(x * x, axis=-1, keepdims=True) + RMS_EPS) * g


def _ffn_kernel(x_ref, g_ref, wg_ref, wu_ref, wd_ref, *rest, emit_x, emit_norm, h_in_output):
    if emit_norm:
        g2_ref, rest = rest[0], rest[1:]
    outs, h_ref = (rest, rest[-1]) if h_in_output else (rest[:-1], rest[-1])
    y_ref = outs[0]
    f = pl.program_id(1)

    @pl.when(f == 0)
    def _():
        x = x_ref[...]
        h_ref[...] = _rms(x, g_ref[...]).astype(BF16)
        y_ref[...] = x

    h = h_ref[...]
    gate = _mm(h, wg_ref[...])
    up = _mm(h, wu_ref[...])
    act = gate * jax.nn.sigmoid(gate) * up
    y_ref[...] += 0.5 * _mm(act.astype(BF16), wd_ref[...])

    if emit_norm:
        @pl.when(f == pl.num_programs(1) - 1)
        def _():
            outs[-1][...] = _rms(y_ref[...], g2_ref[...]).astype(outs[-1].dtype)


def ffn(x, g, w_gate, w_up, w_down, layer, post_gain=None, post_dtype=None, emit_x=True, tm=1024, tf=512):
    t, d = x.shape
    n_chunks = w_gate.shape[2] // tf
    emit_norm = post_gain is not None
    assert emit_x or post_dtype == F32
    row_spec = pl.BlockSpec((tm, d), lambda i, f: (i, 0))
    vec_spec = pl.BlockSpec((1, d), lambda i, f: (0, 0))
    in_specs = [row_spec, vec_spec,
                pl.BlockSpec((None, d, tf), lambda i, f: (layer, 0, f)),
                pl.BlockSpec((None, d, tf), lambda i, f: (layer, 0, f)),
                pl.BlockSpec((None, tf, d), lambda i, f: (layer, f, 0))]
    args = [x, g.reshape(1, d), w_gate, w_up, w_down]
    out_specs, out_shape = [], []
    if emit_x:
        out_specs.append(row_spec)
        out_shape.append(jax.ShapeDtypeStruct((t, d), F32))
    if emit_norm:
        in_specs.append(vec_spec)
        args.append(post_gain.reshape(1, d))
        out_specs.append(row_spec)
        out_shape.append(jax.ShapeDtypeStruct((t, d), post_dtype))
    h_in_output = emit_x and emit_norm and post_dtype == BF16
    res = pl.pallas_call(
        functools.partial(_ffn_kernel, emit_x=emit_x, emit_norm=emit_norm, h_in_output=h_in_output),
        grid=(t // tm, n_chunks),
        in_specs=in_specs,
        out_specs=out_specs,
        out_shape=out_shape,
        scratch_shapes=[] if h_in_output else [pltpu.VMEM((tm, d), BF16)],
        compiler_params=_params("parallel", "arbitrary", vmem_limit_bytes=FFN_VMEM_LIMIT_BYTES),
        name="ffn",
    )(*args)
    return res if len(res) > 1 else res[0]


def _proj_kernel(a_ref, *refs, mode, scale, tn, n_wblocks):
    w_refs, refs = refs[:n_wblocks], refs[n_wblocks:]
    if scale == "row":
        scale_ref, refs = refs[0], refs[1:]
    if mode in ("rot128", "rot256"):
        c_ref, s_ref, o_ref = refs
        c, s = c_ref[...], s_ref[...]
    elif mode == "residual":
        x_ref, o_ref = refs
    else:
        (o_ref,) = refs
    a = a_ref[...]
    wb = tn // n_wblocks
    for u, w_ref in enumerate(w_refs):
        cols = slice(u * wb, (u + 1) * wb)
        y = _mm(a, w_ref[...].astype(BF16))
        if scale is not None:
            y = y * (scale_ref[:, cols] if scale == "row" else scale)
        if mode == "rot128":
            for b in range(wb // LANES):
                yb = y[:, b * LANES:(b + 1) * LANES]
                ob = yb * c + pltpu.roll(yb, LANES // 2, axis=1) * s
                o_ref[:, u * wb + b * LANES:u * wb + (b + 1) * LANES] = ob.astype(o_ref.dtype)
        elif mode == "rot256":
            for b in range(wb // (2 * LANES)):
                y1 = y[:, (2 * b) * LANES:(2 * b + 1) * LANES]
                y2 = y[:, (2 * b + 1) * LANES:(2 * b + 2) * LANES]
                lo = u * wb + 2 * b * LANES
                o_ref[:, lo:lo + LANES] = (y1 * c - y2 * s).astype(o_ref.dtype)
                o_ref[:, lo + LANES:lo + 2 * LANES] = (y2 * c + y1 * s).astype(o_ref.dtype)
        elif mode == "sigmoid":
            o_ref[:, cols] = _sigmoid(y).astype(o_ref.dtype)
        elif mode == "residual":
            o_ref[:, cols] = (x_ref[:, cols] + y).astype(o_ref.dtype)
        else:
            o_ref[:, cols] = y.astype(o_ref.dtype)


def proj(a, w, layer, col_off, n_cols, mode, out_dtype, *, tables=None, scale=None, resid=None, seq=None,
         tm=2048, wb=512, n_wblocks=1):
    t, k = a.shape
    tn = wb * n_wblocks
    jb = col_off // wb
    assert col_off % wb == 0 and n_cols % tn == 0
    in_specs = [pl.BlockSpec((tm, k), lambda i, j: (i, 0))]
    in_specs += [pl.BlockSpec((None, k, wb), lambda i, j, u=u: (layer, 0, jb + j * n_wblocks + u))
                 for u in range(n_wblocks)]
    args = [a] + [w] * n_wblocks
    if scale is not None and not isinstance(scale, float):
        in_specs.append(pl.BlockSpec((1, tn), lambda i, j: (0, j)))
        args.append(scale)
        scale = "row"
    if mode in ("rot128", "rot256"):
        nsb = seq // tm
        tab_spec = pl.BlockSpec((tm, LANES), lambda i, j: (i % nsb, 0))
        in_specs += [tab_spec, tab_spec]
        args += list(tables)
    elif mode == "residual":
        in_specs.append(pl.BlockSpec((tm, tn), lambda i, j: (i, j)))
        args.append(resid)
    return pl.pallas_call(
        functools.partial(_proj_kernel, mode=mode, scale=scale, tn=tn, n_wblocks=n_wblocks),
        grid=(t // tm, n_cols // tn),
        in_specs=in_specs,
        out_specs=pl.BlockSpec((tm, tn), lambda i, j: (i, j)),
        out_shape=jax.ShapeDtypeStruct((t, n_cols), out_dtype),
        compiler_params=_params("parallel", "arbitrary"),
        name="proj_" + mode,
    )(*args)


def _attn_kernel(q0, q1, q2, k0, k1, k2, v0, v1, v2, kp0, kp1, kp2, vp0, vp1, vp2,
                 o_ref, a0, a1, a2, m0, m1, m2, l0, l1, l2):
    row = lax.broadcasted_iota(jnp.int32, (BLOCK, 2 * BLOCK), 0)
    col = lax.broadcasted_iota(jnp.int32, (BLOCK, 2 * BLOCK), 1)
    band = jnp.logical_and(col >= row, col <= row + BLOCK)
    band_halo = jnp.logical_and(band, col >= jnp.where(pl.program_id(2) > 0, 0, BLOCK))
    ones = jnp.ones((2 * BLOCK, LANES), BF16)

    def tiles(items, a_ref, m_ref, l_ref):
        s = [jnp.where(mask, _mm_nt(q.astype(BF16), jnp.concatenate([kp, kc], axis=0).astype(BF16)), NEG_INF)
             for _, q, kp, kc, _, _, mask in items]
        m = [jnp.max(jnp.maximum(si[:, :BLOCK], si[:, BLOCK:]), axis=1, keepdims=True) for si in s]
        p = [jnp.where(it[6], jnp.exp(si - mi), 0.0).astype(BF16) for it, si, mi in zip(items, s, m)]
        al = [_mm(pi, jnp.concatenate([jnp.concatenate([vp, vc], axis=0).astype(BF16), ones], axis=1))
              for (_, _, _, _, vp, vc, _), pi in zip(items, p)]
        for (rows, *_), ali, mi in zip(items, al, m):
            a_ref[rows, :] = ali[:, :LANES]
            l_ref[rows, :] = ali[:, LANES:]
            m_ref[rows, :] = jnp.broadcast_to(mi, (BLOCK, LANES))

    def group(q, k, v, kp, vp, a_ref, m_ref, l_ref, d):
        span = BLOCK * d
        n_sb = ATTN_SUPER // span
        u = ATTN_UNROLL

        def rows_at(start):
            return pl.ds(start, BLOCK, stride=d) if d > 1 else pl.ds(start, BLOCK)

        def item(start, halo):
            rows = rows_at(start)
            if halo:
                return (rows, q[rows, :], kp[rows, :], k[rows, :], vp[rows, :], v[rows, :], band_halo)
            prev = rows_at(start - span)
            return (rows, q[rows, :], k[prev, :], k[rows, :], v[prev, :], v[rows, :], band)

        def batch(items):
            tiles(items, a_ref, m_ref, l_ref)

        if d >= u:
            per = d // u

            def halo_batch(i, carry):
                batch([item(i * u + j, True) for j in range(u)])
                return carry

            lax.fori_loop(0, per, halo_batch, 0)

            def later(sb, carry):
                for rb in range(per):
                    batch([item(sb * span + rb * u + j, False) for j in range(u)])
                return carry

            if n_sb > 1:
                lax.fori_loop(1, n_sb, later, 0)
        else:
            spb = u // d
            batch([item((j // d) * span + j % d, j < d) for j in range(u)])

            def later(i, carry):
                base = pl.multiple_of(i * (spb * span), spb * span)
                batch([item(base + (j // d) * span + j % d, False) for j in range(u)])
                return carry

            if n_sb > spb:
                lax.fori_loop(1, n_sb // spb, later, 0)

    group(q0, k0, v0, kp0, vp0, a0, m0, l0, ATTN_GROUPS[0][1])
    group(q1, k1, v1, kp1, vp1, a1, m1, l1, ATTN_GROUPS[1][1])
    group(q2, k2, v2, kp2, vp2, a2, m2, l2, ATTN_GROUPS[2][1])

    rc = 256

    def merge(i, carry):
        rows = pl.ds(pl.multiple_of(i * rc, rc), rc)
        ma, mb, mc = m0[rows, :], m1[rows, :], m2[rows, :]
        mx = jnp.maximum(jnp.maximum(ma, mb), mc)
        ea, eb, ec = jnp.exp(ma - mx), jnp.exp(mb - mx), jnp.exp(mc - mx)
        num = ea * a0[rows, :] + eb * a1[rows, :] + ec * a2[rows, :]
        den = ea * l0[rows, :] + eb * l1[rows, :] + ec * l2[rows, :]
        o_ref[rows, :] = (num / den).astype(o_ref.dtype)
        return carry

    lax.fori_loop(0, ATTN_SUPER // rc, merge, 0)


def attention(qk, v, out_dtype):
    b, s, _ = v.shape
    hg = ATTN_HEADS_PER_GROUP
    nsb = s // ATTN_SUPER

    def cur(g, col0):
        return pl.BlockSpec((None, ATTN_SUPER, HEAD_DIM), lambda bi, h, j: (bi, j, col0 + g * hg + h))

    def prev(g, col0):
        span = BLOCK * ATTN_GROUPS[g][1]
        per = ATTN_SUPER // span
        return pl.BlockSpec((None, span, HEAD_DIM),
                            lambda bi, h, j: (bi, jnp.maximum(j * per - 1, 0), col0 + g * hg + h))

    groups, k0 = range(len(ATTN_GROUPS)), N_ATTN_HEADS
    in_specs = ([cur(g, 0) for g in groups] + [cur(g, k0) for g in groups] + [cur(g, 0) for g in groups]
                + [prev(g, k0) for g in groups] + [prev(g, 0) for g in groups])
    scratch = [pltpu.VMEM((ATTN_SUPER, LANES), F32) for _ in range(9)]
    return pl.pallas_call(
        _attn_kernel,
        grid=(b, hg, nsb),
        in_specs=in_specs,
        out_specs=pl.BlockSpec((None, ATTN_SUPER, HEAD_DIM), lambda bi, h, j: (bi, j, h)),
        out_shape=jax.ShapeDtypeStruct((b, s, ATTN_OUT_WIDTH), out_dtype),
        scratch_shapes=scratch,
        compiler_params=_params("parallel", "parallel", "arbitrary"),
        name="dilated_attn",
    )(*([qk] * 6 + [v] * 3 + [qk] * 3 + [v] * 3))


def _ret_kernel(q_ref, k_ref, v_ref, hn_ref, wg_ref, dm_ref, qd_ref, kd_ref, cd_ref, o_ref,
                state_ref, kv_ref, sbf_ref, gate_ref, *, n_chunks):
    @pl.when(pl.program_id(2) == 0)
    def _():
        state_ref[...] = jnp.zeros_like(state_ref)

    c_len = RET_CHUNK
    rows = [pl.ds(c * c_len, c_len) for c in range(n_chunks)]

    dmask, qdec, kdec, cdec = dm_ref[...], qd_ref[...], kd_ref[...], cd_ref[...]
    kd_t = [(k_ref[r, :].astype(F32) * kdec).T.astype(BF16) for r in rows]
    for c, r in enumerate(rows):
        kv_ref[c] = _mm(kd_t[c], v_ref[r, :])
    scores = [_mm_nt(q_ref[r, :], k_ref[r, :]) for r in rows]
    gate_ref[...] = _mm(hn_ref[...], wg_ref[...].astype(BF16))

    for i in range(RET_QK_DIM // RET_SLAB):
        rs = pl.ds(i * RET_SLAB, RET_SLAB)
        st = state_ref[rs, :]
        for c in range(n_chunks):
            sbf_ref[c, rs, :] = st.astype(BF16)
            st = st * cdec + kv_ref[c, rs, :]
        state_ref[rs, :] = st

    for c, r in enumerate(rows):
        lhs = jnp.concatenate([(scores[c] * dmask).astype(BF16),
                               (q_ref[r, :].astype(F32) * qdec).astype(BF16)], axis=1)
        rhs = jnp.concatenate([v_ref[r, :], sbf_ref[c]], axis=0)
        o = _mm(lhs, rhs)
        mu = jnp.mean(o, axis=-1, keepdims=True)
        oc = o - mu
        var = jnp.mean(oc * oc, axis=-1, keepdims=True)
        on = oc * lax.rsqrt(var + GN_EPS)
        g = gate_ref[r, :]
        o_ref[r, :] = (on * (g * _sigmoid(g))).astype(o_ref.dtype)


def _ret_tables():
    c_len = RET_CHUNK
    log_gamma = jnp.log1p(-jnp.exp2(-5.0 - jnp.arange(N_RET_HEADS, dtype=F32)))
    idx = jnp.arange(c_len, dtype=F32)
    rel = idx[:, None] - idx[None, :]
    dmask = jnp.where(rel >= 0, jnp.exp(log_gamma[:, None, None] * jnp.maximum(rel, 0.0)), 0.0)
    qdec = jnp.exp(log_gamma[:, None] * (idx + 1.0)[None, :])
    kdec = jnp.exp(log_gamma[:, None] * (c_len - 1.0 - idx)[None, :])
    cdec = jnp.exp(log_gamma * c_len)
    qdec = jnp.broadcast_to(qdec[:, :, None], (N_RET_HEADS, c_len, RET_QK_DIM))
    kdec = jnp.broadcast_to(kdec[:, :, None], (N_RET_HEADS, c_len, RET_QK_DIM))
    cdec = jnp.broadcast_to(cdec[:, None, None], (N_RET_HEADS, 1, RET_V_DIM))
    return dmask, qdec, kdec, cdec


def retention(q, k, v, hn, w_in, layer, gate_off, tables, out_dtype, tc=2048):
    b, s, _ = q.shape
    d = hn.shape[2]
    n_chunks = tc // RET_CHUNK
    gate_blk = gate_off // RET_V_DIM
    assert gate_off % RET_V_DIM == 0
    qk_spec = pl.BlockSpec((None, tc, RET_QK_DIM), lambda bi, h, j: (bi, j, h))
    v_spec = pl.BlockSpec((None, tc, RET_V_DIM), lambda bi, h, j: (bi, j, h))
    hn_spec = pl.BlockSpec((None, tc, d), lambda bi, h, j: (bi, j, 0))
    wg_spec = pl.BlockSpec((None, d, RET_V_DIM), lambda bi, h, j: (layer, 0, gate_blk + h))

    def tab(shape):
        return pl.BlockSpec((None,) + shape, lambda bi, h, j: (h, 0, 0))

    return pl.pallas_call(
        functools.partial(_ret_kernel, n_chunks=n_chunks),
        grid=(b, N_RET_HEADS, s // tc),
        in_specs=[qk_spec, qk_spec, v_spec, hn_spec, wg_spec,
                  tab((RET_CHUNK, RET_CHUNK)), tab((RET_CHUNK, RET_QK_DIM)), tab((RET_CHUNK, RET_QK_DIM)),
                  tab((1, RET_V_DIM))],
        out_specs=v_spec,
        out_shape=jax.ShapeDtypeStruct((b, s, RET_V_WIDTH), out_dtype),
        scratch_shapes=[pltpu.VMEM((RET_QK_DIM, RET_V_DIM), F32),
                        pltpu.VMEM((n_chunks, RET_QK_DIM, RET_V_DIM), F32),
                        pltpu.VMEM((n_chunks, RET_QK_DIM, RET_V_DIM), BF16),
                        pltpu.VMEM((tc, RET_V_DIM), F32)],
        compiler_params=_params("parallel", "parallel", "arbitrary"),
        name="retention",
    )(q, k, v, hn, w_in, *tables)


def _merge_kernel(oa_ref, or_ref, pa_ref, pr_ref, ga_ref, gb_ref, o_ref):
    ya = _mm(oa_ref[...], pa_ref[...])
    yr = _mm(or_ref[...], pr_ref[...])
    merged = ga_ref[...].astype(F32) * ya + gb_ref[...].astype(F32) * yr
    o_ref[...] = merged.astype(o_ref.dtype)


def merge(o_a, o_r, p_a, p_r, layer, gates, out_dtype, tm=1024, tn=512):
    t, wa = o_a.shape
    wr = o_r.shape[1]
    d = p_a.shape[2]
    nb = d // tn
    return pl.pallas_call(
        _merge_kernel,
        grid=(t // tm, nb),
        in_specs=[pl.BlockSpec((tm, wa), lambda i, j: (i, 0)),
                  pl.BlockSpec((tm, wr), lambda i, j: (i, 0)),
                  pl.BlockSpec((None, wa, tn), lambda i, j: (layer, 0, j)),
                  pl.BlockSpec((None, wr, tn), lambda i, j: (layer, 0, j)),
                  pl.BlockSpec((tm, tn), lambda i, j: (i, j)),
                  pl.BlockSpec((tm, tn), lambda i, j: (i, nb + j))],
        out_specs=pl.BlockSpec((tm, tn), lambda i, j: (i, j)),
        out_shape=jax.ShapeDtypeStruct((t, d), out_dtype),
        compiler_params=_params("parallel", "arbitrary"),
        name="gated_merge",
    )(o_a, o_r, p_a, p_r, gates, gates)


def _rope_tables(seq, dim):
    inv_freq = 1.0 / (ROPE_THETA ** (jnp.arange(0, dim, 2, dtype=F32) / dim))
    ang = jnp.arange(seq, dtype=F32)[:, None] * inv_freq[None, :]
    return jnp.cos(ang), jnp.sin(ang)


def kernel(x, ffn1_norm, ffn1_w_gate, ffn1_w_up, ffn1_w_down, mix_norm, w_in, w_proj_attn, w_proj_ret,
           w_out, ffn2_norm, ffn2_w_gate, ffn2_w_up, ffn2_w_down, final_norm):
    b, s, d = x.shape
    t = b * s
    depth = w_in.shape[0]
    cos_a, sin_a = _rope_tables(s, HEAD_DIM)
    tab_a = (jnp.concatenate([cos_a, cos_a], axis=1), jnp.concatenate([-sin_a, sin_a], axis=1))
    tab_r = _rope_tables(s, RET_QK_DIM)
    ret_tabs = _ret_tables()
    off = IN_OFFSETS
    qk_scale = jnp.concatenate([jnp.full((1, ATTN_WIDTH), HEAD_DIM ** -0.5, F32), jnp.ones((1, ATTN_WIDTH), F32)],
                               axis=1)
    bf = lambda w: w.astype(BF16)
    f1 = (bf(ffn1_w_gate), bf(ffn1_w_up), bf(ffn1_w_down))
    f2 = (bf(ffn2_w_gate), bf(ffn2_w_up), bf(ffn2_w_down))
    p_a, p_r, w_o = bf(w_proj_attn), bf(w_proj_ret), bf(w_out)

    x = x.reshape(t, d)
    for layer in range(depth):
        x, h = ffn(x, ffn1_norm[layer], *f1, layer, post_gain=mix_norm[layer], post_dtype=BF16)

        def pj(seg, width, mode, dtype, **kw):
            return proj(h, w_in, layer, off[seg], width, mode, dtype, seq=s, **kw)

        wide, big = dict(tm=1024, n_wblocks=2), dict(tm=2048, n_wblocks=2)
        qk_a = pj(0, 2 * ATTN_WIDTH, "rot128", F32, tables=tab_a, scale=qk_scale, **wide)
        v_a = pj(2, ATTN_WIDTH, "none", F32)
        q_r = pj(3, RET_QK_WIDTH, "rot256", BF16, tables=tab_r, **big)
        k_r = pj(4, RET_QK_WIDTH, "rot256", BF16, tables=tab_r, scale=RET_QK_DIM ** -0.5, **big)
        v_r = pj(5, RET_V_WIDTH, "none", BF16, **big)
        gates = pj(7, 2 * D_MODEL, "sigmoid", BF16, **big)
        o_a = attention(qk_a.reshape(b, s, -1), v_a.reshape(b, s, -1), BF16)
        o_r = retention(q_r.reshape(b, s, -1), k_r.reshape(b, s, -1), v_r.reshape(b, s, -1),
                        h.reshape(b, s, -1), w_in, layer, off[6], ret_tabs, BF16)
        merged = merge(o_a.reshape(t, -1), o_r.reshape(t, -1), p_a, p_r, layer, gates, BF16)
        x = proj(merged, w_o, layer, 0, D_MODEL, "residual", F32, resid=x, tm=512, wb=D_MODEL)
        last = layer == depth - 1
        x = ffn(x, ffn2_norm[layer], *f2, layer, post_gain=final_norm if last else None, post_dtype=F32,
                emit_x=not last)
    return x.reshape(b, s, d)
```

```python
import functools

import jax
import jax.numpy as jnp
import numpy as np
from jax import lax
from jax.experimental import pallas as pl
from jax.experimental.pallas import tpu as pltpu

F32 = jnp.float32
BF16 = jnp.bfloat16

LANES = 128
VMEM_LIMIT_BYTES = 56 * 1024 * 1024
FFN_VMEM_LIMIT_BYTES = 60 * 1024 * 1024

D_MODEL = 2048
HEAD_DIM = 128
ATTN_GROUPS = ((128, 1), (512, 4), (2048, 16))
ATTN_HEADS_PER_GROUP = 4
N_ATTN_HEADS = len(ATTN_GROUPS) * ATTN_HEADS_PER_GROUP
ATTN_WIDTH = N_ATTN_HEADS * HEAD_DIM
ATTN_OUT_WIDTH = ATTN_HEADS_PER_GROUP * HEAD_DIM
BLOCK = 128
ROPE_THETA = 10000.0
NEG_INF = -1e30
N_RET_HEADS = D_MODEL // 256
RET_QK_DIM = 256
RET_V_DIM = 512
RET_QK_WIDTH = N_RET_HEADS * RET_QK_DIM
RET_V_WIDTH = N_RET_HEADS * RET_V_DIM
GN_EPS = 1e-5
RMS_EPS = 1e-6
IN_SPLITS = (ATTN_WIDTH, ATTN_WIDTH, ATTN_WIDTH, RET_QK_WIDTH, RET_QK_WIDTH,
             RET_V_WIDTH, RET_V_WIDTH, D_MODEL, D_MODEL)
IN_OFFSETS = tuple(int(o) for o in np.cumsum((0,) + IN_SPLITS[:-1]))

ATTN_SUPER = BLOCK * ATTN_GROUPS[-1][1]
ATTN_UNROLL = 8
RET_CHUNK = 256
RET_SLAB = 32


def _params(*sem, vmem_limit_bytes=VMEM_LIMIT_BYTES):
    return pltpu.CompilerParams(dimension_semantics=sem, vmem_limit_bytes=vmem_limit_bytes)


def _mm(a, b):
    return jnp.dot(a, b, preferred_element_type=F32)


def _sigmoid(x):
    return 0.5 * jnp.tanh(0.5 * x) + 0.5


def _mm_nt(a, b):
    return lax.dot_general(a, b, (((1,), (1,)), ((), ())), preferred_element_type=F32)


def _rms(x, g):
    return x * lax.rsqrt(jnp.mean(x * x, axis=-1, keepdims=True) + RMS_EPS) * g


def _ffn_kernel(x_ref, g_ref, wg_ref, wu_ref, wd_ref, *rest, emit_x, emit_norm, h_in_output):
    if emit_norm:
        g2_ref, rest = rest[0], rest[1:]
    outs, h_ref = (rest, rest[-1]) if h_in_output else (rest[:-1], rest[-1])
    y_ref = outs[0]
    f = pl.program_id(1)

    @pl.when(f == 0)
    def _():
        x = x_ref[...]
        h_ref[...] = _rms(x, g_ref[...]).astype(BF16)
        y_ref[...] = x

    h = h_ref[...]
    gate = _mm(h, wg_ref[...])
    up = _mm(h, wu_ref[...])
    act = gate * jax.nn.sigmoid(gate) * up
    y_ref[...] += 0.5 * _mm(act.astype(BF16), wd_ref[...])

    if emit_norm:
        @pl.when(f == pl.num_programs(1) - 1)
        def _():
            outs[-1][...] = _rms(y_ref[...], g2_ref[...]).astype(outs[-1].dtype)


def ffn(x, g, w_gate, w_up, w_down, layer, post_gain=None, post_dtype=None, emit_x=True, tm=1024, tf=512):
    t, d = x.shape
    n_chunks = w_gate.shape[2] // tf
    emit_norm = post_gain is not None
    assert emit_x or post_dtype == F32
    row_spec = pl.BlockSpec((tm, d), lambda i, f: (i, 0))
    vec_spec = pl.BlockSpec((1, d), lambda i, f: (0, 0))
    in_specs = [row_spec, vec_spec,
                pl.BlockSpec((None, d, tf), lambda i, f: (layer, 0, f)),
                pl.BlockSpec((None, d, tf), lambda i, f: (layer, 0, f)),
                pl.BlockSpec((None, tf, d), lambda i, f: (layer, f, 0))]
    args = [x, g.reshape(1, d), w_gate, w_up, w_down]
    out_specs, out_shape = [], []
    if emit_x:
        out_specs.append(row_spec)
        out_shape.append(jax.ShapeDtypeStruct((t, d), F32))
    if emit_norm:
        in_specs.append(vec_spec)
        args.append(post_gain.reshape(1, d))
        out_specs.append(row_spec)
        out_shape.append(jax.ShapeDtypeStruct((t, d), post_dtype))
    h_in_output = emit_x and emit_norm and post_dtype == BF16
    res = pl.pallas_call(
        functools.partial(_ffn_kernel, emit_x=emit_x, emit_norm=emit_norm, h_in_output=h_in_output),
        grid=(t // tm, n_chunks),
        in_specs=in_specs,
        out_specs=out_specs,
        out_shape=out_shape,
        scratch_shapes=[] if h_in_output else [pltpu.VMEM((tm, d), BF16)],
        compiler_params=_params("parallel", "arbitrary", vmem_limit_bytes=FFN_VMEM_LIMIT_BYTES),
        name="ffn",
    )(*args)
    return res if len(res) > 1 else res[0]


def _proj_kernel(a_ref, *refs, mode, scale, tn, n_wblocks):
    w_refs, refs = refs[:n_wblocks], refs[n_wblocks:]
    if scale == "row":
        scale_ref, refs = refs[0], refs[1:]
    if mode in ("rot128", "rot256"):
        c_ref, s_ref, o_ref = refs
        c, s = c_ref[...], s_ref[...]
    elif mode == "residual":
        x_ref, o_ref = refs
    else:
        (o_ref,) = refs
    a = a_ref[...]
    wb = tn // n_wblocks
    for u, w_ref in enumerate(w_refs):
        cols = slice(u * wb, (u + 1) * wb)
        y = _mm(a, w_ref[...].astype(BF16))
        if scale is not None:
            y = y * (scale_ref[:, cols] if scale == "row" else scale)
        if mode == "rot128":
            for b in range(wb // LANES):
                yb = y[:, b * LANES:(b + 1) * LANES]
                ob = yb * c + pltpu.roll(yb, LANES // 2, axis=1) * s
                o_ref[:, u * wb + b * LANES:u * wb + (b + 1) * LANES] = ob.astype(o_ref.dtype)
        elif mode == "rot256":
            for b in range(wb // (2 * LANES)):
                y1 = y[:, (2 * b) * LANES:(2 * b + 1) * LANES]
                y2 = y[:, (2 * b + 1) * LANES:(2 * b + 2) * LANES]
                lo = u * wb + 2 * b * LANES
                o_ref[:, lo:lo + LANES] = (y1 * c - y2 * s).astype(o_ref.dtype)
                o_ref[:, lo + LANES:lo + 2 * LANES] = (y2 * c + y1 * s).astype(o_ref.dtype)
        elif mode == "sigmoid":
            o_ref[:, cols] = _sigmoid(y).astype(o_ref.dtype)
        elif mode == "residual":
            o_ref[:, cols] = (x_ref[:, cols] + y).astype(o_ref.dtype)
        else:
            o_ref[:, cols] = y.astype(o_ref.dtype)


def proj(a, w, layer, col_off, n_cols, mode, out_dtype, *, tables=None, scale=None, resid=None, seq=None,
         tm=2048, wb=512, n_wblocks=1):
    t, k = a.shape
    tn = wb * n_wblocks
    jb = col_off // wb
    assert col_off % wb == 0 and n_cols % tn == 0
    in_specs = [pl.BlockSpec((tm, k), lambda i, j: (i, 0))]
    in_specs += [pl.BlockSpec((None, k, wb), lambda i, j, u=u: (layer, 0, jb + j * n_wblocks + u))
                 for u in range(n_wblocks)]
    args = [a] + [w] * n_wblocks
    if scale is not None and not isinstance(scale, float):
        in_specs.append(pl.BlockSpec((1, tn), lambda i, j: (0, j)))
        args.append(scale)
        scale = "row"
    if mode in ("rot128", "rot256"):
        nsb = seq // tm
        tab_spec = pl.BlockSpec((tm, LANES), lambda i, j: (i % nsb, 0))
        in_specs += [tab_spec, tab_spec]
        args += list(tables)
    elif mode == "residual":
        in_specs.append(pl.BlockSpec((tm, tn), lambda i, j: (i, j)))
        args.append(resid)
    return pl.pallas_call(
        functools.partial(_proj_kernel, mode=mode, scale=scale, tn=tn, n_wblocks=n_wblocks),
        grid=(t // tm, n_cols // tn),
        in_specs=in_specs,
        out_specs=pl.BlockSpec((tm, tn), lambda i, j: (i, j)),
        out_shape=jax.ShapeDtypeStruct((t, n_cols), out_dtype),
        compiler_params=_params("parallel", "arbitrary"),
        name="proj_" + mode,
    )(*args)


def _attn_kernel(q0, q1, q2, k0, k1, k2, v0, v1, v2, kp0, kp1, kp2, vp0, vp1, vp2,
                 o_ref, a0, a1, a2, m0, m1, m2, l0, l1, l2):
    row = lax.broadcasted_iota(jnp.int32, (BLOCK, 2 * BLOCK), 0)
    col = lax.broadcasted_iota(jnp.int32, (BLOCK, 2 * BLOCK), 1)
    band = jnp.logical_and(col >= row, col <= row + BLOCK)
    band_halo = jnp.logical_and(band, col >= jnp.where(pl.program_id(2) > 0, 0, BLOCK))
    ones = jnp.ones((2 * BLOCK, LANES), BF16)

    def tiles(items, a_ref, m_ref, l_ref):
        s = [jnp.where(mask, _mm_nt(q.astype(BF16), jnp.concatenate([kp, kc], axis=0).astype(BF16)), NEG_INF)
             for _, q, kp, kc, _, _, mask in items]
        m = [jnp.max(jnp.maximum(si[:, :BLOCK], si[:, BLOCK:]), axis=1, keepdims=True) for si in s]
        p = [jnp.where(it[6], jnp.exp(si - mi), 0.0).astype(BF16) for it, si, mi in zip(items, s, m)]
        al = [_mm(pi, jnp.concatenate([jnp.concatenate([vp, vc], axis=0).astype(BF16), ones], axis=1))
              for (_, _, _, _, vp, vc, _), pi in zip(items, p)]
        for (rows, *_), ali, mi in zip(items, al, m):
            a_ref[rows, :] = ali[:, :LANES]
            l_ref[rows, :] = ali[:, LANES:]
            m_ref[rows, :] = jnp.broadcast_to(mi, (BLOCK, LANES))

    def group(q, k, v, kp, vp, a_ref, m_ref, l_ref, d):
        span = BLOCK * d
        n_sb = ATTN_SUPER // span
        u = ATTN_UNROLL

        def rows_at(start):
            return pl.ds(start, BLOCK, stride=d) if d > 1 else pl.ds(start, BLOCK)

        def item(start, halo):
            rows = rows_at(start)
            if halo:
                return (rows, q[rows, :], kp[rows, :], k[rows, :], vp[rows, :], v[rows, :], band_halo)
            prev = rows_at(start - span)
            return (rows, q[rows, :], k[prev, :], k[rows, :], v[prev, :], v[rows, :], band)

        def batch(items):
            tiles(items, a_ref, m_ref, l_ref)

        if d >= u:
            per = d // u

            def halo_batch(i, carry):
                batch([item(i * u + j, True) for j in range(u)])
                return carry

            lax.fori_loop(0, per, halo_batch, 0)

            def later(sb, carry):
                for rb in range(per):
                    batch([item(sb * span + rb * u + j, False) for j in range(u)])
                return carry

            if n_sb > 1:
                lax.fori_loop(1, n_sb, later, 0)
        else:
            spb = u // d
            batch([item((j // d) * span + j % d, j < d) for j in range(u)])

            def later(i, carry):
                base = pl.multiple_of(i * (spb * span), spb * span)
                batch([item(base + (j // d) * span + j % d, False) for j in range(u)])
                return carry

            if n_sb > spb:
                lax.fori_loop(1, n_sb // spb, later, 0)

    group(q0, k0, v0, kp0, vp0, a0, m0, l0, ATTN_GROUPS[0][1])
    group(q1, k1, v1, kp1, vp1, a1, m1, l1, ATTN_GROUPS[1][1])
    group(q2, k2, v2, kp2, vp2, a2, m2, l2, ATTN_GROUPS[2][1])

    rc = 256

    def merge(i, carry):
        rows = pl.ds(pl.multiple_of(i * rc, rc), rc)
        ma, mb, mc = m0[rows, :], m1[rows, :], m2[rows, :]
        mx = jnp.maximum(jnp.maximum(ma, mb), mc)
        ea, eb, ec = jnp.exp(ma - mx), jnp.exp(mb - mx), jnp.exp(mc - mx)
        num = ea * a0[rows, :] + eb * a1[rows, :] + ec * a2[rows, :]
        den = ea * l0[rows, :] + eb * l1[rows, :] + ec * l2[rows, :]
        o_ref[rows, :] = (num / den).astype(o_ref.dtype)
        return carry

    lax.fori_loop(0, ATTN_SUPER // rc, merge, 0)


def attention(qk, v, out_dtype):
    b, s, _ = v.shape
    hg = ATTN_HEADS_PER_GROUP
    nsb = s // ATTN_SUPER

    def cur(g, col0):
        return pl.BlockSpec((None, ATTN_SUPER, HEAD_DIM), lambda bi, h, j: (bi, j, col0 + g * hg + h))

    def prev(g, col0):
        span = BLOCK * ATTN_GROUPS[g][1]
        per = ATTN_SUPER // span
        return pl.BlockSpec((None, span, HEAD_DIM),
                            lambda bi, h, j: (bi, jnp.maximum(j * per - 1, 0), col0 + g * hg + h))

    groups, k0 = range(len(ATTN_GROUPS)), N_ATTN_HEADS
    in_specs = ([cur(g, 0) for g in groups] + [cur(g, k0) for g in groups] + [cur(g, 0) for g in groups]
                + [prev(g, k0) for g in groups] + [prev(g, 0) for g in groups])
    scratch = [pltpu.VMEM((ATTN_SUPER, LANES), F32) for _ in range(9)]
    return pl.pallas_call(
        _attn_kernel,
        grid=(b, hg, nsb),
        in_specs=in_specs,
        out_specs=pl.BlockSpec((None, ATTN_SUPER, HEAD_DIM), lambda bi, h, j: (bi, j, h)),
        out_shape=jax.ShapeDtypeStruct((b, s, ATTN_OUT_WIDTH), out_dtype),
        scratch_shapes=scratch,
        compiler_params=_params("parallel", "parallel", "arbitrary"),
        name="dilated_attn",
    )(*([qk] * 6 + [v] * 3 + [qk] * 3 + [v] * 3))


def _ret_kernel(q_ref, k_ref, v_ref, hn_ref, wg_ref, dm_ref, qd_ref, kd_ref, cd_ref, o_ref,
                state_ref, kv_ref, sbf_ref, gate_ref, *, n_chunks):
    @pl.when(pl.program_id(2) == 0)
    def _():
        state_ref[...] = jnp.zeros_like(state_ref)

    c_len = RET_CHUNK
    rows = [pl.ds(c * c_len, c_len) for c in range(n_chunks)]

    dmask, qdec, kdec, cdec = dm_ref[...], qd_ref[...], kd_ref[...], cd_ref[...]
    kd_t = [(k_ref[r, :].astype(F32) * kdec).T.astype(BF16) for r in rows]
    for c, r in enumerate(rows):
        kv_ref[c] = _mm(kd_t[c], v_ref[r, :])
    scores = [_mm_nt(q_ref[r, :], k_ref[r, :]) for r in rows]
    gate_ref[...] = _mm(hn_ref[...], wg_ref[...].astype(BF16))

    for i in range(RET_QK_DIM // RET_SLAB):
        rs = pl.ds(i * RET_SLAB, RET_SLAB)
        st = state_ref[rs, :]
        for c in range(n_chunks):
            sbf_ref[c, rs, :] = st.astype(BF16)
            st = st * cdec + kv_ref[c, rs, :]
        state_ref[rs, :] = st

    for c, r in enumerate(rows):
        lhs = jnp.concatenate([(scores[c] * dmask).astype(BF16),
                               (q_ref[r, :].astype(F32) * qdec).astype(BF16)], axis=1)
        rhs = jnp.concatenate([v_ref[r, :], sbf_ref[c]], axis=0)
        o = _mm(lhs, rhs)
        mu = jnp.mean(o, axis=-1, keepdims=True)
        oc = o - mu
        var = jnp.mean(oc * oc, axis=-1, keepdims=True)
        on = oc * lax.rsqrt(var + GN_EPS)
        g = gate_ref[r, :]
        o_ref[r, :] = (on * (g * _sigmoid(g))).astype(o_ref.dtype)


def _ret_tables():
    c_len = RET_CHUNK
    log_gamma = jnp.log1p(-jnp.exp2(-5.0 - jnp.arange(N_RET_HEADS, dtype=F32)))
    idx = jnp.arange(c_len, dtype=F32)
    rel = idx[:, None] - idx[None, :]
    dmask = jnp.where(rel >= 0, jnp.exp(log_gamma[:, None, None] * jnp.maximum(rel, 0.0)), 0.0)
    qdec = jnp.exp(log_gamma[:, None] * (idx + 1.0)[None, :])
    kdec = jnp.exp(log_gamma[:, None] * (c_len - 1.0 - idx)[None, :])
    cdec = jnp.exp(log_gamma * c_len)
    qdec = jnp.broadcast_to(qdec[:, :, None], (N_RET_HEADS, c_len, RET_QK_DIM))
    kdec = jnp.broadcast_to(kdec[:, :, None], (N_RET_HEADS, c_len, RET_QK_DIM))
    cdec = jnp.broadcast_to(cdec[:, None, None], (N_RET_HEADS, 1, RET_V_DIM))
    return dmask, qdec, kdec, cdec


def retention(q, k, v, hn, w_in, layer, gate_off, tables, out_dtype, tc=2048):
    b, s, _ = q.shape
    d = hn.shape[2]
    n_chunks = tc // RET_CHUNK
    gate_blk = gate_off // RET_V_DIM
    assert gate_off % RET_V_DIM == 0
    qk_spec = pl.BlockSpec((None, tc, RET_QK_DIM), lambda bi, h, j: (bi, j, h))
    v_spec = pl.BlockSpec((None, tc, RET_V_DIM), lambda bi, h, j: (bi, j, h))
    hn_spec = pl.BlockSpec((None, tc, d), lambda bi, h, j: (bi, j, 0))
    wg_spec = pl.BlockSpec((None, d, RET_V_DIM), lambda bi, h, j: (layer, 0, gate_blk + h))

    def tab(shape):
        return pl.BlockSpec((None,) + shape, lambda bi, h, j: (h, 0, 0))

    return pl.pallas_call(
        functools.partial(_ret_kernel, n_chunks=n_chunks),
        grid=(b, N_RET_HEADS, s // tc),
        in_specs=[qk_spec, qk_spec, v_spec, hn_spec, wg_spec,
                  tab((RET_CHUNK, RET_CHUNK)), tab((RET_CHUNK, RET_QK_DIM)), tab((RET_CHUNK, RET_QK_DIM)),
                  tab((1, RET_V_DIM))],
        out_specs=v_spec,
        out_shape=jax.ShapeDtypeStruct((b, s, RET_V_WIDTH), out_dtype),
        scratch_shapes=[pltpu.VMEM((RET_QK_DIM, RET_V_DIM), F32),
                        pltpu.VMEM((n_chunks, RET_QK_DIM, RET_V_DIM), F32),
                        pltpu.VMEM((n_chunks, RET_QK_DIM, RET_V_DIM), BF16),
                        pltpu.VMEM((tc, RET_V_DIM), F32)],
        compiler_params=_params("parallel", "parallel", "arbitrary"),
        name="retention",
    )(q, k, v, hn, w_in, *tables)


def _merge_kernel(oa_ref, or_ref, pa_ref, pr_ref, ga_ref, gb_ref, o_ref):
    ya = _mm(oa_ref[...], pa_ref[...])
    yr = _mm(or_ref[...], pr_ref[...])
    merged = ga_ref[...].astype(F32) * ya + gb_ref[...].astype(F32) * yr
    o_ref[...] = merged.astype(o_ref.dtype)


def merge(o_a, o_r, p_a, p_r, layer, gates, out_dtype, tm=1024, tn=512):
    t, wa = o_a.shape
    wr = o_r.shape[1]
    d = p_a.shape[2]
    nb = d // tn
    return pl.pallas_call(
        _merge_kernel,
        grid=(t // tm, nb),
        in_specs=[pl.BlockSpec((tm, wa), lambda i, j: (i, 0)),
                  pl.BlockSpec((tm, wr), lambda i, j: (i, 0)),
                  pl.BlockSpec((None, wa, tn), lambda i, j: (layer, 0, j)),
                  pl.BlockSpec((None, wr, tn), lambda i, j: (layer, 0, j)),
                  pl.BlockSpec((tm, tn), lambda i, j: (i, j)),
                  pl.BlockSpec((tm, tn), lambda i, j: (i, nb + j))],
        out_specs=pl.BlockSpec((tm, tn), lambda i, j: (i, j)),
        out_shape=jax.ShapeDtypeStruct((t, d), out_dtype),
        compiler_params=_params("parallel", "arbitrary"),
        name="gated_merge",
    )(o_a, o_r, p_a, p_r, gates, gates)


def _rope_tables(seq, dim):
    inv_freq = 1.0 / (ROPE_THETA ** (jnp.arange(0, dim, 2, dtype=F32) / dim))
    ang = jnp.arange(seq, dtype=F32)[:, None] * inv_freq[None, :]
    return jnp.cos(ang), jnp.sin(ang)


def kernel(x, ffn1_norm, ffn1_w_gate, ffn1_w_up, ffn1_w_down, mix_norm, w_in, w_proj_attn, w_proj_ret,
           w_out, ffn2_norm, ffn2_w_gate, ffn2_w_up, ffn2_w_down, final_norm):
    b, s, d = x.shape
    t = b * s
    depth = w_in.shape[0]
    cos_a, sin_a = _rope_tables(s, HEAD_DIM)
    tab_a = (jnp.concatenate([cos_a, cos_a], axis=1), jnp.concatenate([-sin_a, sin_a], axis=1))
    tab_r = _rope_tables(s, RET_QK_DIM)
    ret_tabs = _ret_tables()
    off = IN_OFFSETS
    qk_scale = jnp.concatenate([jnp.full((1, ATTN_WIDTH), HEAD_DIM ** -0.5, F32), jnp.ones((1, ATTN_WIDTH), F32)],
                               axis=1)
    bf = lambda w: w.astype(BF16)
    f1 = (bf(ffn1_w_gate), bf(ffn1_w_up), bf(ffn1_w_down))
    f2 = (bf(ffn2_w_gate), bf(ffn2_w_up), bf(ffn2_w_down))
    p_a, p_r, w_o = bf(w_proj_attn), bf(w_proj_ret), bf(w_out)
    w_attn = bf(w_in[:, :, :off[3]])

    x = x.reshape(t, d)
    for layer in range(depth):
        x, h = ffn(x, ffn1_norm[layer], *f1, layer, post_gain=mix_norm[layer], post_dtype=BF16)

        def pj(seg, width, mode, dtype, w=w_in, **kw):
            return proj(h, w, layer, off[seg], width, mode, dtype, seq=s, **kw)

        big = dict(tm=2048, n_wblocks=2)
        qk_a = pj(0, 2 * ATTN_WIDTH, "rot128", F32, w=w_attn, tables=tab_a, scale=qk_scale, **big)
        v_a = pj(2, ATTN_WIDTH, "none", F32, w=w_attn)
        q_r = pj(3, RET_QK_WIDTH, "rot256", BF16, tables=tab_r, **big)
        k_r = pj(4, RET_QK_WIDTH, "rot256", BF16, tables=tab_r, scale=RET_QK_DIM ** -0.5, **big)
        v_r = pj(5, RET_V_WIDTH, "none", BF16, **big)
        gates = pj(7, 2 * D_MODEL, "sigmoid", BF16, **big)
        o_a = attention(qk_a.reshape(b, s, -1), v_a.reshape(b, s, -1), BF16)
        o_r = retention(q_r.reshape(b, s, -1), k_r.reshape(b, s, -1), v_r.reshape(b, s, -1),
                        h.reshape(b, s, -1), w_in, layer, off[6], ret_tabs, BF16)
        merged = merge(o_a.reshape(t, -1), o_r.reshape(t, -1), p_a, p_r, layer, gates, BF16)
        x = proj(merged, w_o, layer, 0, D_MODEL, "residual", F32, resid=x, tm=512, wb=D_MODEL)
        last = layer == depth - 1
        x = ffn(x, ffn2_norm[layer], *f2, layer, post_gain=final_norm if last else None, post_dtype=F32,
                emit_x=not last)
    return x.reshape(b, s, d)
```
